```python
import math
import jax, jax.numpy as jnp
from jax import lax
import numpy as np

D_MODEL = 2048
BATCH = 4
SEQ = 2048
DEPTH = 1
DEC_BATCH = 128
DEC_SEQ = 8
PAST_LEN = 2048
PAGE_SIZE = 128

ATT_GROUPS = ((128, 1), (512, 4), (2048, 16))
N_GROUPS = 3
ATT_HEADS = 8
ATT_HD = 128
ATT_W = N_GROUPS * ATT_HEADS * ATT_HD
ATT_OUT = ATT_HEADS * ATT_HD
ROPE_THETA = 10000.0
Q_BLOCK = 128
RET_HEADS = 8
RET_DK = 128
RET_DV = 256
RET_QK_W = RET_HEADS * RET_DK
RET_V_W = RET_HEADS * RET_DV
RET_CHUNK = 128
PEER_HEADS = 8
PEER_NKEYS = 128
PEER_EXPERTS = PEER_NKEYS * PEER_NKEYS
PEER_DKEY = 256
PEER_TOPK = 16
PEER_TOK_BLOCK = 256
EPS = 1e-6
NEG = -1e30
IN_W = 3 * ATT_W + 2 * RET_QK_W + 2 * RET_V_W + 2 * D_MODEL

kernel_name = "hybrid_dilated_attn_retention_peer_step"


def rmsnorm(x, w=None):
    xf = x.astype(jnp.float32)
    y = xf * lax.rsqrt(jnp.mean(xf * xf, axis=-1, keepdims=True) + EPS)
    if w is not None:
        y = y * w.astype(jnp.float32)
    return y.astype(x.dtype)


def rope(x, pos):
    half = x.shape[-1] // 2
    inv_freq = jnp.power(jnp.float32(ROPE_THETA), -jnp.arange(half, dtype=jnp.float32) / half)
    ang = pos.astype(jnp.float32)[:, None] * inv_freq[None, :]
    bshape = (pos.shape[0],) + (1,) * (x.ndim - 3) + (half,)
    cos = jnp.cos(ang).reshape(bshape)
    sin = jnp.sin(ang).reshape(bshape)
    xf = x.astype(jnp.float32)
    x1, x2 = xf[..., :half], xf[..., half:]
    return jnp.concatenate([x1 * cos - x2 * sin, x2 * cos + x1 * sin], axis=-1).astype(x.dtype)


def dilated_group_attn(q, k_ctx, v_ctx, window, dil, start_pos):
    B, Lq, H, D = q.shape
    n_keys = window // dil + 1
    blk = math.gcd(Lq, Q_BLOCK)
    nb = Lq // blk
    j = jnp.arange(n_keys, dtype=jnp.int32)
    scale = D ** -0.5

    def block(b):
        i = b * blk + jnp.arange(blk, dtype=jnp.int32)
        rel = i[:, None] - dil * j[None, :]
        idx = window + rel
        valid = (start_pos + rel) >= 0
        kg = jnp.take(k_ctx, idx, axis=1)
        vg = jnp.take(v_ctx, idx, axis=1)
        qb = lax.dynamic_slice_in_dim(q, b * blk, blk, axis=1)
        s = jnp.einsum('bqhd,bqkhd->bqhk', qb, kg, preferred_element_type=jnp.float32) * scale
        s = jnp.where(valid[None, :, None, :], s, NEG)
        m = jnp.max(s, axis=-1, keepdims=True)
        p = jnp.exp(s - m)
        den = jnp.sum(p, axis=-1, keepdims=True)
        o = jnp.einsum('bqhk,bqkhd->bqhd', (p / den).astype(vg.dtype), vg)
        lse = (m + jnp.log(den))[..., 0]
        return o, lse

    o, lse = lax.map(block, jnp.arange(nb))
    o = jnp.moveaxis(o, 0, 1).reshape(B, Lq, H, D)
    lse = jnp.moveaxis(lse, 0, 1).reshape(B, Lq, H)
    return o, lse


def retention(q, k, v, s0):
    B, L, H, DK = q.shape
    DV = v.shape[-1]
    C = math.gcd(L, RET_CHUNK)
    nc = L // C
    log_g = jnp.log1p(-jnp.exp2(-5.0 - jnp.arange(H, dtype=jnp.float32)))
    ii = jnp.arange(C, dtype=jnp.float32)
    diff = ii[:, None] - ii[None, :]
    inner = jnp.where(diff[None] >= 0, jnp.exp(jnp.maximum(diff, 0.0)[None] * log_g[:, None, None]), 0.0)
    q_dec = jnp.exp((ii + 1.0)[:, None] * log_g[None, :])
    k_dec = jnp.exp((C - 1.0 - ii)[:, None] * log_g[None, :])
    chunk_dec = jnp.exp(C * log_g)

    def to_chunks(a):
        return jnp.swapaxes(a.astype(jnp.float32).reshape(B, nc, C, H, a.shape[-1]), 0, 1)

    def step(S, inp):
        qb, kb, vb = inp
        sc = jnp.einsum('bihd,bjhd->bhij', qb, kb) * inner[None]
        o = jnp.einsum('bhij,bjhe->bihe', sc, vb) + jnp.einsum('bihd,bhde->bihe', qb * q_dec[None, :, :, None], S)
        S = S * chunk_dec[None, :, None, None] + jnp.einsum('bjhd,bjhe->bhde', kb * k_dec[None, :, :, None], vb)
        return S, o

    S, o = lax.scan(step, s0.astype(jnp.float32), (to_chunks(q), to_chunks(k), to_chunks(v)))
    o = jnp.swapaxes(o, 0, 1).reshape(B, L, H, DV)
    return o.astype(v.dtype), S.astype(s0.dtype)


def token_mixer(xn, start_pos, att_past, ret_state, w_in, att_qn, att_kn, w_att_br, w_ret_br, w_mix_out):
    B, L, _ = xn.shape
    sizes = (ATT_W, ATT_W, ATT_W, RET_QK_W, RET_QK_W, RET_V_W, RET_V_W, D_MODEL, D_MODEL)
    cuts = [sum(sizes[:i + 1]) for i in range(len(sizes) - 1)]
    q_a, k_a, v_a, q_r, k_r, v_r, g_r, gate_a, gate_b = jnp.split(xn @ w_in, cuts, axis=-1)
    pos = start_pos + jnp.arange(L, dtype=jnp.int32)
    ga = (B, L, N_GROUPS, ATT_HEADS, ATT_HD)
    q_a = rope(rmsnorm(q_a.reshape(ga), att_qn), pos)
    k_a = rope(rmsnorm(k_a.reshape(ga), att_kn), pos)
    v_a = v_a.reshape(ga)
    outs, lses, kv_rows = [], [], []
    for g, (win, dil) in enumerate(ATT_GROUPS):
        past_k, past_v = att_past[g]
        o, lse = dilated_group_attn(q_a[:, :, g],
                                    jnp.concatenate([past_k, k_a[:, :, g]], axis=1),
                                    jnp.concatenate([past_v, v_a[:, :, g]], axis=1),
                                    win, dil, start_pos)
        outs.append(o)
        lses.append(lse)
        kv_rows.append(jnp.stack([k_a[:, :, g], v_a[:, :, g]], axis=2))
    alpha = jax.nn.softmax(jnp.stack(lses, axis=0), axis=0)
    o_att = jnp.einsum('gblh,gblhd->blhd', alpha.astype(xn.dtype), jnp.stack(outs, axis=0))
    q_r = rope(q_r.reshape(B, L, RET_HEADS, RET_DK), pos)
    k_r = rope(k_r.reshape(B, L, RET_HEADS, RET_DK), pos) * (RET_DK ** -0.5)
    v_r = v_r.reshape(B, L, RET_HEADS, RET_DV)
    o_ret, new_state = retention(q_r, k_r, v_r, ret_state)
    o_ret = rmsnorm(o_ret).reshape(B, L, RET_V_W) * jax.nn.silu(g_r)
    a_br = o_att.reshape(B, L, ATT_OUT) @ w_att_br
    b_br = o_ret @ w_ret_br
    mixed = jax.nn.sigmoid(gate_a) * a_br + jax.nn.sigmoid(gate_b) * b_br
    return mixed @ w_mix_out, kv_rows, new_state


def peer(xn, w_q, sub_keys, u_tab, v_tab):
    T, D = xn.shape
    blk = min(PEER_TOK_BLOCK, T)
    nb = -(-T // blk)
    xp = jnp.pad(xn, ((0, nb * blk - T), (0, 0))).reshape(nb, blk, D)
    half = PEER_DKEY // 2

    def block(xb):
        q = (xb @ w_q).reshape(blk, PEER_HEADS, 2, half)
        s = jnp.einsum('thpc,hpnc->thpn', q, sub_keys, preferred_element_type=jnp.float32)
        s_top, i_top = lax.top_k(s, PEER_TOPK)
        cand = s_top[:, :, 0, :, None] + s_top[:, :, 1, None, :]
        cand_idx = i_top[:, :, 0, :, None] * PEER_NKEYS + i_top[:, :, 1, None, :]
        c_s, c_i = lax.top_k(cand.reshape(blk, PEER_HEADS, PEER_TOPK * PEER_TOPK), PEER_TOPK)
        e_idx = jnp.take_along_axis(cand_idx.reshape(blk, PEER_HEADS, PEER_TOPK * PEER_TOPK), c_i, axis=-1)
        gate = jax.nn.softmax(c_s, axis=-1)
        u = jnp.take(u_tab, e_idx, axis=0)
        v = jnp.take(v_tab, e_idx, axis=0)
        act = jax.nn.gelu(jnp.einsum('td,thkd->thk', xb, u))
        return jnp.einsum('thk,thkd->td', (gate * act).astype(v.dtype), v)

    return lax.map(block, xp).reshape(nb * blk, D)[:T]


def decoder_layer(x, start_pos, att_past, ret_state, norm1_w, w_in, att_qn, att_kn, w_att_br, w_ret_br,
                  w_mix_out, norm2_w, peer_w_q, peer_sub_keys, peer_u, peer_v):
    mix, kv_rows, new_state = token_mixer(rmsnorm(x, norm1_w), start_pos, att_past, ret_state, w_in,
                                          att_qn, att_kn, w_att_br, w_ret_br, w_mix_out)
    h = x + mix
    hn = rmsnorm(h, norm2_w)
    y = h + peer(hn.reshape(-1, D_MODEL), peer_w_q, peer_sub_keys, peer_u, peer_v).reshape(h.shape)
    return y, kv_rows, new_state


def setup_inputs(seed: int = 0) -> dict:
    key = jax.random.key(seed)
    ks = jax.random.split(key, 20)
    f32 = jnp.float32

    def nrm(k, shape, scale):
        return jax.random.normal(k, shape, f32) * scale

    wbuf = [min(win, PAST_LEN) for win, _ in ATT_GROUPS]
    return {
        "x_prompt": nrm(ks[0], (BATCH, SEQ, D_MODEL), 1.0),
        "x_sample": nrm(ks[1], (DEC_BATCH, DEC_SEQ, D_MODEL), 1.0),
        "cache_att_w128": nrm(ks[2], (DEPTH, DEC_BATCH, wbuf[0], 2, ATT_HEADS, ATT_HD), 1.0),
        "cache_att_w512": nrm(ks[3], (DEPTH, DEC_BATCH, wbuf[1], 2, ATT_HEADS, ATT_HD), 1.0),
        "cache_att_w2048": nrm(ks[4], (DEPTH, DEC_BATCH, wbuf[2], 2, ATT_HEADS, ATT_HD), 1.0),
        "state_ret": nrm(ks[5], (DEPTH, DEC_BATCH, RET_HEADS, RET_DK, RET_DV), 0.3),
        "norm1_w": 1.0 + nrm(ks[6], (DEPTH, D_MODEL), 0.02),
        "w_in": nrm(ks[7], (DEPTH, D_MODEL, IN_W), D_MODEL ** -0.5),
        "att_q_norm_w": 1.0 + nrm(ks[8], (DEPTH, ATT_HD), 0.02),
        "att_k_norm_w": 1.0 + nrm(ks[9], (DEPTH, ATT_HD), 0.02),
        "w_att_br": nrm(ks[10], (DEPTH, ATT_OUT, D_MODEL), ATT_OUT ** -0.5),
        "w_ret_br": nrm(ks[11], (DEPTH, RET_V_W, D_MODEL), RET_V_W ** -0.5),
        "w_mix_out": nrm(ks[12], (DEPTH, D_MODEL, D_MODEL), D_MODEL ** -0.5),
        "norm2_w": 1.0 + nrm(ks[13], (DEPTH, D_MODEL), 0.02),
        "peer_w_q": nrm(ks[14], (DEPTH, D_MODEL, PEER_HEADS * PEER_DKEY), D_MODEL ** -0.5),
        "peer_sub_keys": nrm(ks[15], (DEPTH, PEER_HEADS, 2, PEER_NKEYS, PEER_DKEY // 2), (PEER_DKEY // 2) ** -0.5),
        "peer_u": nrm(ks[16], (DEPTH, PEER_EXPERTS, D_MODEL), D_MODEL ** -0.5),
        "peer_v": nrm(ks[17], (DEPTH, PEER_EXPERTS, D_MODEL), PEER_HEADS ** -0.5),
    }


def reference(x_prompt, x_sample, cache_att_w128, cache_att_w512, cache_att_w2048, state_ret,
              norm1_w, w_in, att_q_norm_w, att_k_norm_w, w_att_br, w_ret_br, w_mix_out,
              norm2_w, peer_w_q, peer_sub_keys, peer_u, peer_v):
    caches = (cache_att_w128, cache_att_w512, cache_att_w2048)
    B, L, _ = x_prompt.shape
    yp, ys = x_prompt, x_sample
    p_kv = [[], [], []]
    s_kv = [[], [], []]
    p_ret, s_ret = [], []
    for l in range(DEPTH):
        past_p = [(jnp.zeros((B, win, ATT_HEADS, ATT_HD), x_prompt.dtype),
                   jnp.zeros((B, win, ATT_HEADS, ATT_HD), x_prompt.dtype)) for win, _ in ATT_GROUPS]
        s0_p = jnp.zeros((B, RET_HEADS, RET_DK, RET_DV), jnp.float32)
        yp, kv_p, st_p = decoder_layer(yp, 0, past_p, s0_p, norm1_w[l], w_in[l], att_q_norm_w[l], att_k_norm_w[l],
                                       w_att_br[l], w_ret_br[l], w_mix_out[l], norm2_w[l], peer_w_q[l],
                                       peer_sub_keys[l], peer_u[l], peer_v[l])
        past_s = []
        for g, (win, _) in enumerate(ATT_GROUPS):
            c = caches[g][l]
            c = jnp.pad(c, ((0, 0), (win - c.shape[1], 0), (0, 0), (0, 0), (0, 0)))
            past_s.append((c[:, :, 0], c[:, :, 1]))
        ys, kv_s, st_s = decoder_layer(ys, PAST_LEN, past_s, state_ret[l], norm1_w[l], w_in[l], att_q_norm_w[l],
                                       att_k_norm_w[l], w_att_br[l], w_ret_br[l], w_mix_out[l], norm2_w[l],
                                       peer_w_q[l], peer_sub_keys[l], peer_u[l], peer_v[l])
        for g, (win, _) in enumerate(ATT_GROUPS):
            p_kv[g].append(kv_p[g][:, L - min(win, L):])
            s_kv[g].append(kv_s[g])
        p_ret.append(st_p)
        s_ret.append(st_s)
    return (yp, ys,
            jnp.stack(p_kv[0]), jnp.stack(p_kv[1]), jnp.stack(p_kv[2]), jnp.stack(p_ret),
            jnp.stack(s_kv[0]), jnp.stack(s_kv[1]), jnp.stack(s_kv[2]), jnp.stack(s_ret))
```

```python
import functools
import math

import jax
import jax.numpy as jnp
from jax import lax
from jax.experimental import pallas as pl
from jax.experimental.pallas import tpu as pltpu

F32 = jnp.float32
BF16 = jnp.bfloat16

ATT_GROUPS = ((128, 1), (512, 4), (2048, 16))
N_GROUPS = 3
ATT_HEADS = 8
HD = 128
ATT_W = N_GROUPS * ATT_HEADS * HD
ATT_OUT = ATT_HEADS * HD
N_KEYS = 129
ROPE_THETA = 10000.0
RET_HEADS = 8
RET_DK = 128
RET_DV = 256
RET_QK_W = RET_HEADS * RET_DK
RET_V_W = RET_HEADS * RET_DV
RET_CHUNK = 128
PEER_HEADS = 8
PEER_NKEYS = 128
PEER_TOPK = 16
EPS = 1e-6
NEG = -1e30
F32_MIN = float(jnp.finfo(jnp.float32).min)

OFF_QA = 0
OFF_KA = ATT_W
OFF_VA = 2 * ATT_W
OFF_QR = 3 * ATT_W
OFF_KR = OFF_QR + RET_QK_W
OFF_VR = OFF_KR + RET_QK_W
OFF_GR = OFF_VR + RET_V_W
OFF_GA = OFF_GR + RET_V_W

VMEM_LIMIT = 56 * 1024 * 1024


def _cparams(sem):
    return pltpu.CompilerParams(dimension_semantics=sem, vmem_limit_bytes=VMEM_LIMIT)


def _sigmoid(x):
    return 1.0 / (1.0 + jnp.exp(-x))


def _rope_tables(pos, width):
    half = width // 2
    inv_freq = jnp.power(jnp.float32(ROPE_THETA), -jnp.arange(half, dtype=F32) / half)
    ang = pos.astype(F32)[:, None] * inv_freq[None, :]
    cos, sin = jnp.cos(ang), jnp.sin(ang)
    return jnp.concatenate([cos, cos], axis=-1), jnp.concatenate([-sin, sin], axis=-1)


def _rope(x, cos, sin):
    return x * cos + pltpu.roll(x, HD // 2, 1) * sin


def _head_rmsnorm(x, w):
    ms = jnp.mean(x * x, axis=-1, keepdims=True)
    return x * lax.rsqrt(ms + EPS) * w


def _norm_matmul_kernel(x_ref, nw_ref, w_ref, o_ref, xn_ref):
    @pl.when(pl.program_id(1) == 0)
    def _():
        x = x_ref[...]
        ms = jnp.mean(x * x, axis=-1, keepdims=True)
        xn_ref[...] = (x * lax.rsqrt(ms + EPS) * nw_ref[...]).astype(BF16)

    o_ref[...] = jnp.dot(xn_ref[...], w_ref[...], preferred_element_type=F32)


def _norm_matmul(x, norm_w, w_bf16, tm, tn):
    t, d = x.shape
    n = w_bf16.shape[1]
    tm = min(tm, t)
    return pl.pallas_call(
        _norm_matmul_kernel,
        grid=(t // tm, n // tn),
        in_specs=[
            pl.BlockSpec((tm, d), lambda i, j: (i, 0)),
            pl.BlockSpec((1, d), lambda i, j: (0, 0)),
            pl.BlockSpec((d, tn), lambda i, j: (0, j)),
        ],
        out_specs=pl.BlockSpec((tm, tn), lambda i, j: (i, j)),
        out_shape=jax.ShapeDtypeStruct((t, n), F32),
        scratch_shapes=[pltpu.VMEM((tm, d), BF16)],
        compiler_params=_cparams(("parallel", "arbitrary")),
        name="norm_inproj",
    )(x, norm_w.reshape(1, d), w_bf16)


def _attn_prompt_kernel(q0, k0, v0, q1, k1, v1, q2, k2, v2, cos_ref, sin_ref, qn_ref, kn_ref,
                        ko0, vo0, ko1, vo1, ko2, vo2, o_ref,
                        q_s, k_s, v_s, og_s, lse_s):
    L = q0.shape[0]
    blk = 128
    scale = HD ** -0.5
    cos = cos_ref[...]
    sin = sin_ref[...]
    k_s[pl.ds(0, L), :] = jnp.zeros((L, HD), F32)
    v_s[pl.ds(0, L), :] = jnp.zeros((L, HD), F32)
    groups = ((q0, k0, v0, ko0, vo0), (q1, k1, v1, ko1, vo1), (q2, k2, v2, ko2, vo2))
    for g, (qr, kr, vr, ko, vo) in enumerate(groups):
        win, dil = ATT_GROUPS[g]
        q = _rope(_head_rmsnorm(qr[...], qn_ref[...]), cos, sin) * scale
        k = _rope(_head_rmsnorm(kr[...], kn_ref[...]), cos, sin)
        v = vr[...]
        ko[...] = k
        vo[...] = v
        q_s[...] = q
        k_s[pl.ds(L, L), :] = k
        v_s[pl.ds(L, L), :] = v
        log_d = int(math.log2(dil))

        def body(t, carry, dil=dil, log_d=log_d, g=g):
            r = jnp.bitwise_and(t, dil - 1)
            i = lax.shift_right_logical(t, log_d)
            qstart = r + dil * blk * i
            kstart = L + qstart - dil * blk
            if dil == 1:
                qsl = pl.ds(qstart, blk)
                ksl = pl.ds(kstart, 2 * blk)
            else:
                qsl = pl.ds(qstart, blk, stride=dil)
                ksl = pl.ds(kstart, 2 * blk, stride=dil)
            qb = q_s[qsl, :].astype(BF16)
            kb = k_s[ksl, :].astype(BF16)
            vb = v_s[ksl, :].astype(BF16)
            s = lax.dot_general(qb, kb, (((1,), (1,)), ((), ())), preferred_element_type=F32)
            qi = lax.broadcasted_iota(jnp.int32, (blk, 2 * blk), 0)
            kj = lax.broadcasted_iota(jnp.int32, (blk, 2 * blk), 1)
            lo = jnp.maximum(qi, jnp.where(i == 0, blk, 0))
            valid = jnp.logical_and(kj >= lo, kj <= qi + blk)
            s = jnp.where(valid, s, NEG)
            m = jnp.max(s, axis=-1, keepdims=True)
            p = jnp.exp(s - m)
            den = jnp.sum(p, axis=-1, keepdims=True)
            o = jnp.dot((p / den).astype(BF16), vb, preferred_element_type=F32)
            lse = m + jnp.log(den)
            osl = pl.ds(g * L + qstart, blk) if dil == 1 else pl.ds(g * L + qstart, blk, stride=dil)
            og_s[osl, :] = o
            lse_s[osl, :] = jnp.broadcast_to(lse, (blk, HD))
            return carry

        lax.fori_loop(0, L // blk, body, 0)

    l0 = lse_s[pl.ds(0, L), :]
    l1 = lse_s[pl.ds(L, L), :]
    l2 = lse_s[pl.ds(2 * L, L), :]
    mx = jnp.maximum(jnp.maximum(l0, l1), l2)
    e0 = jnp.exp(l0 - mx)
    e1 = jnp.exp(l1 - mx)
    e2 = jnp.exp(l2 - mx)
    tot = e0 + e1 + e2
    o = ((e0 / tot) * og_s[pl.ds(0, L), :] + (e1 / tot) * og_s[pl.ds(L, L), :]
         + (e2 / tot) * og_s[pl.ds(2 * L, L), :])
    o_ref[...] = o.astype(o_ref.dtype)


def _attn_prompt(proj, cos, sin, qn, kn):
    b, l, _ = proj.shape
    in_specs = []
    for g in range(N_GROUPS):
        for off in (OFF_QA, OFF_KA, OFF_VA):
            cb = (off + g * ATT_OUT) // HD
            in_specs.append(pl.BlockSpec((None, l, HD), lambda bi, hi, cb=cb: (bi, 0, cb + hi)))
    in_specs += [
        pl.BlockSpec((l, HD), lambda bi, hi: (0, 0)),
        pl.BlockSpec((l, HD), lambda bi, hi: (0, 0)),
        pl.BlockSpec((1, HD), lambda bi, hi: (0, 0)),
        pl.BlockSpec((1, HD), lambda bi, hi: (0, 0)),
    ]
    head_spec = pl.BlockSpec((None, l, HD), lambda bi, hi: (bi, 0, hi))
    kv_shape = jax.ShapeDtypeStruct((b, l, ATT_OUT), F32)
    outs = pl.pallas_call(
        _attn_prompt_kernel,
        grid=(b, ATT_HEADS),
        in_specs=in_specs,
        out_specs=[head_spec] * 7,
        out_shape=[kv_shape] * 6 + [jax.ShapeDtypeStruct((b, l, ATT_OUT), BF16)],
        scratch_shapes=[
            pltpu.VMEM((l, HD), F32),
            pltpu.VMEM((2 * l, HD), F32),
            pltpu.VMEM((2 * l, HD), F32),
            pltpu.VMEM((N_GROUPS * l, HD), F32),
            pltpu.VMEM((N_GROUPS * l, HD), F32),
        ],
        compiler_params=_cparams(("parallel", "arbitrary")),
        name="attn_prompt",
    )(*([proj] * 9), cos, sin, qn, kn)
    return outs[:6], outs[6]


def _attn_sample_kernel(qa_ref, ka_ref, va_ref, c0_ref, c1_ref, c2_ref, cos_ref, sin_ref,
                        qn_ref, kn_ref, kv0_ref, kv1_ref, kv2_ref, o_ref, kbuf, vbuf):
    nq = qa_ref.shape[0]
    scale = HD ** -0.5
    cos = cos_ref[...]
    sin = sin_ref[...]
    qn = qn_ref[...]
    kn = kn_ref[...]
    rows = ATT_HEADS * nq
    bd_r = lax.broadcasted_iota(jnp.int32, (rows, ATT_OUT), 0)
    bd_c = lax.broadcasted_iota(jnp.int32, (rows, ATT_OUT), 1)
    blockdiag = (lax.shift_right_logical(bd_r, int(math.log2(nq)))
                 == lax.shift_right_logical(bd_c, int(math.log2(HD))))
    caches = (c0_ref, c1_ref, c2_ref)
    kv_refs = (kv0_ref, kv1_ref, kv2_ref)
    o_groups = []
    lse_groups = []
    for g in range(N_GROUPS):
        win, dil = ATT_GROUPS[g]
        qs, ks, vs = [], [], []
        for h in range(ATT_HEADS):
            c0 = g * ATT_OUT + h * HD
            qs.append(_rope(_head_rmsnorm(qa_ref[:, c0:c0 + HD], qn), cos, sin) * scale)
            ks.append(_rope(_head_rmsnorm(ka_ref[:, c0:c0 + HD], kn), cos, sin))
        q_g = jnp.concatenate(qs, axis=1)
        k_g = jnp.concatenate(ks, axis=1)
        v_g = va_ref[:, g * ATT_OUT:(g + 1) * ATT_OUT]
        kv_refs[g][:, 0:ATT_OUT] = k_g
        kv_refs[g][:, ATT_OUT:2 * ATT_OUT] = v_g
        cache = caches[g][...]
        if cache.ndim == 3:
            cache = cache.reshape(cache.shape[0] * cache.shape[1], cache.shape[2])
        wp = cache.shape[0]
        ncol = wp + 128
        kbuf[pl.ds(0, wp), :] = cache[:, 0:ATT_OUT].astype(BF16)
        vbuf[pl.ds(0, wp), :] = cache[:, ATT_OUT:2 * ATT_OUT].astype(BF16)
        zpad = jnp.zeros((128 - nq, ATT_OUT), F32)
        kbuf[pl.ds(wp, 128), :] = jnp.concatenate([k_g, zpad], axis=0).astype(BF16)
        vbuf[pl.ds(wp, 128), :] = jnp.concatenate([v_g, zpad], axis=0).astype(BF16)
        qbd = jnp.where(blockdiag, jnp.concatenate([q_g] * ATT_HEADS, axis=0), 0.0).astype(BF16)
        s = lax.dot_general(qbd, kbuf[pl.ds(0, ncol), :], (((1,), (1,)), ((), ())),
                            preferred_element_type=F32)
        qi = jnp.bitwise_and(lax.broadcasted_iota(jnp.int32, (rows, ncol), 0), nq - 1)
        c = lax.broadcasted_iota(jnp.int32, (rows, ncol), 1)
        if g == 2:
            valid_cache = jnp.bitwise_and(c, 7) == qi
        else:
            valid_cache = jnp.logical_and(c >= qi, jnp.bitwise_and(c - qi, dil - 1) == 0)
        t = c - wp
        valid_new = jnp.logical_and(t <= qi, jnp.bitwise_and(qi - t, dil - 1) == 0)
        valid = jnp.logical_or(jnp.logical_and(c < wp, valid_cache),
                               jnp.logical_and(c >= wp, valid_new))
        s = jnp.where(valid, s, NEG)
        m = jnp.max(s, axis=-1, keepdims=True)
        p = jnp.exp(s - m)
        den = jnp.sum(p, axis=-1, keepdims=True)
        o = jnp.dot((p / den).astype(BF16), vbuf[pl.ds(0, ncol), :], preferred_element_type=F32)
        lse = m + jnp.log(den)
        o_groups.append([o[h * nq:(h + 1) * nq, h * HD:(h + 1) * HD] for h in range(ATT_HEADS)])
        lse_groups.append([lse[h * nq:(h + 1) * nq, :] for h in range(ATT_HEADS)])
    outs = []
    for h in range(ATT_HEADS):
        l0, l1, l2 = lse_groups[0][h], lse_groups[1][h], lse_groups[2][h]
        mx = jnp.maximum(jnp.maximum(l0, l1), l2)
        e0, e1, e2 = jnp.exp(l0 - mx), jnp.exp(l1 - mx), jnp.exp(l2 - mx)
        tot = e0 + e1 + e2
        outs.append((e0 / tot) * o_groups[0][h] + (e1 / tot) * o_groups[1][h]
                    + (e2 / tot) * o_groups[2][h])
    o_ref[...] = jnp.concatenate(outs, axis=1)


def _attn_sample(proj, c0, c1, c2, cos, sin, qn, kn):
    b, nq, _ = proj.shape
    kvw = 2 * ATT_OUT
    small = lambda shape: pl.BlockSpec(shape, lambda bi: (0, 0))
    wp_max = max(c0.shape[1], c1.shape[1], c2.shape[1] * 8) + 128
    outs = pl.pallas_call(
        _attn_sample_kernel,
        grid=(b,),
        in_specs=[
            pl.BlockSpec((None, nq, ATT_W), lambda bi: (bi, 0, OFF_QA // ATT_W)),
            pl.BlockSpec((None, nq, ATT_W), lambda bi: (bi, 0, OFF_KA // ATT_W)),
            pl.BlockSpec((None, nq, ATT_W), lambda bi: (bi, 0, OFF_VA // ATT_W)),
            pl.BlockSpec((None, c0.shape[1], kvw), lambda bi: (bi, 0, 0)),
            pl.BlockSpec((None, c1.shape[1], kvw), lambda bi: (bi, 0, 0)),
            pl.BlockSpec((None, c2.shape[1], 8, kvw), lambda bi: (bi, 0, 0, 0)),
            small((nq, HD)), small((nq, HD)), small((1, HD)), small((1, HD)),
        ],
        out_specs=[pl.BlockSpec((None, nq, kvw), lambda bi: (bi, 0, 0))] * 3
        + [pl.BlockSpec((None, nq, ATT_OUT), lambda bi: (bi, 0, 0))],
        out_shape=[jax.ShapeDtypeStruct((b, nq, kvw), F32)] * 3
        + [jax.ShapeDtypeStruct((b, nq, ATT_OUT), F32)],
        scratch_shapes=[pltpu.VMEM((wp_max, ATT_OUT), BF16), pltpu.VMEM((wp_max, ATT_OUT), BF16)],
        compiler_params=_cparams(("parallel",)),
        name="attn_sample",
    )(proj, proj, proj, c0, c1, c2, cos, sin, qn, kn)
    return outs[:3], outs[3]


def _ret_tables(c):
    log_g = jnp.log1p(-jnp.exp2(-5.0 - jnp.arange(RET_HEADS, dtype=F32)))
    ii = jnp.arange(c, dtype=F32)
    diff = ii[:, None] - ii[None, :]
    inner = jnp.where(diff[None] >= 0, jnp.exp(jnp.maximum(diff, 0.0)[None] * log_g[:, None, None]), 0.0)
    q_dec = jnp.exp((ii + 1.0)[None, :] * log_g[:, None])
    k_dec = jnp.exp((c - 1.0 - ii)[None, :] * log_g[:, None])
    chunk_dec = jnp.exp(c * log_g)
    return inner, q_dec, k_dec, chunk_dec


def _ret_prompt_kernel(q_ref, k_ref, v_ref, g_ref, cos_ref, sin_ref, inner_ref, qd_ref, kd_ref, cd_ref,
                       o_ref, st_ref, q_s, k_s, s_s):
    L = q_ref.shape[0]
    C = RET_CHUNK
    q_s[...] = _rope(q_ref[...], cos_ref[...], sin_ref[...])
    k_s[...] = _rope(k_ref[...], cos_ref[...], sin_ref[...]) * (RET_DK ** -0.5)
    s_s[...] = jnp.zeros(s_s.shape, F32)
    inner = inner_ref[...]
    qd = qd_ref[...]
    kd = kd_ref[...]
    cd = cd_ref[...]

    def body(ci, carry):
        sl = pl.ds(pl.multiple_of(ci * C, C), C)
        qc = q_s[sl, :]
        kc = k_s[sl, :]
        vc = v_ref[sl, :].astype(BF16)
        state = s_s[...]
        sc = lax.dot_general(qc.astype(BF16), kc.astype(BF16), (((1,), (1,)), ((), ())),
                             preferred_element_type=F32) * inner
        o = (jnp.dot(sc.astype(BF16), vc, preferred_element_type=F32)
             + jnp.dot((qc * qd).astype(BF16), state.astype(BF16), preferred_element_type=F32))
        kdt = jnp.transpose(kc * kd).astype(BF16)
        s_s[...] = state * cd + jnp.dot(kdt, vc, preferred_element_type=F32)
        ms = jnp.mean(o * o, axis=-1, keepdims=True)
        gate = g_ref[sl, :]
        o_ref[sl, :] = (o * lax.rsqrt(ms + EPS) * (gate * _sigmoid(gate))).astype(o_ref.dtype)
        return carry

    lax.fori_loop(0, L // C, body, 0)
    st_ref[...] = s_s[...]


def _ret_prompt(proj, cos, sin):
    b, l, _ = proj.shape
    inner, q_dec, k_dec, chunk_dec = _ret_tables(RET_CHUNK)
    qd = jnp.broadcast_to(q_dec[:, :, None], (RET_HEADS, RET_CHUNK, RET_DK))
    kd = jnp.broadcast_to(k_dec[:, :, None], (RET_HEADS, RET_CHUNK, RET_DK))
    cd = jnp.broadcast_to(chunk_dec[:, None, None], (RET_HEADS, RET_DK, RET_DV))
    per_head = lambda r, c: pl.BlockSpec((None, r, c), lambda bi, hi: (hi, 0, 0))
    return pl.pallas_call(
        _ret_prompt_kernel,
        grid=(b, RET_HEADS),
        in_specs=[
            pl.BlockSpec((None, l, RET_DK), lambda bi, hi: (bi, 0, OFF_QR // RET_DK + hi)),
            pl.BlockSpec((None, l, RET_DK), lambda bi, hi: (bi, 0, OFF_KR // RET_DK + hi)),
            pl.BlockSpec((None, l, RET_DV), lambda bi, hi: (bi, 0, OFF_VR // RET_DV + hi)),
            pl.BlockSpec((None, l, RET_DV), lambda bi, hi: (bi, 0, OFF_GR // RET_DV + hi)),
            pl.BlockSpec((l, RET_DK), lambda bi, hi: (0, 0)),
            pl.BlockSpec((l, RET_DK), lambda bi, hi: (0, 0)),
            per_head(RET_CHUNK, RET_CHUNK), per_head(RET_CHUNK, RET_DK), per_head(RET_CHUNK, RET_DK),
            per_head(RET_DK, RET_DV),
        ],
        out_specs=[
            pl.BlockSpec((None, l, RET_DV), lambda bi, hi: (bi, 0, hi)),
            pl.BlockSpec((None, None, RET_DK, RET_DV), lambda bi, hi: (bi, hi, 0, 0)),
        ],
        out_shape=[jax.ShapeDtypeStruct((b, l, RET_V_W), BF16),
                   jax.ShapeDtypeStruct((b, RET_HEADS, RET_DK, RET_DV), F32)],
        scratch_shapes=[pltpu.VMEM((l, RET_DK), F32), pltpu.VMEM((l, RET_DK), F32),
                        pltpu.VMEM((RET_DK, RET_DV), F32)],
        compiler_params=_cparams(("parallel", "arbitrary")),
        name="ret_prompt",
    )(proj, proj, proj, proj, cos, sin, inner, qd, kd, cd)


def _ret_sample_kernel(a_ref, b_ref, s0_ref, cos_ref, sin_ref, inner_ref, qd_ref, kd_ref, cd_ref,
                       o_ref, st_ref):
    nq = a_ref.shape[0]
    cos = cos_ref[...]
    sin = sin_ref[...]
    pad = 128 - nq
    for h in range(RET_HEADS):
        q = _rope(a_ref[:, h * RET_DK:(h + 1) * RET_DK], cos, sin)
        k = _rope(a_ref[:, RET_QK_W + h * RET_DK:RET_QK_W + (h + 1) * RET_DK], cos, sin) * (RET_DK ** -0.5)
        vc0 = h * RET_DV
        if vc0 < ATT_OUT:
            v = a_ref[:, 2 * RET_QK_W + vc0:2 * RET_QK_W + vc0 + RET_DV]
        else:
            v = b_ref[:, vc0 - ATT_OUT:vc0 - ATT_OUT + RET_DV]
        gate = b_ref[:, ATT_OUT + h * RET_DV:ATT_OUT + (h + 1) * RET_DV]
        state = s0_ref[h]
        kp = jnp.concatenate([k, jnp.zeros((pad, RET_DK), F32)], axis=0)
        vp = jnp.concatenate([v, jnp.zeros((pad, RET_DV), F32)], axis=0).astype(BF16)
        sc = lax.dot_general(q.astype(BF16), kp.astype(BF16), (((1,), (1,)), ((), ())),
                             preferred_element_type=F32) * inner_ref[h]
        o = (jnp.dot(sc.astype(BF16), vp, preferred_element_type=F32)
             + jnp.dot((q * qd_ref[h]).astype(BF16), state.astype(BF16), preferred_element_type=F32))
        kdt = jnp.transpose(kp * kd_ref[h]).astype(BF16)
        st_ref[h] = state * cd_ref[h] + jnp.dot(kdt, vp, preferred_element_type=F32)
        ms = jnp.mean(o * o, axis=-1, keepdims=True)
        o_ref[:, h * RET_DV:(h + 1) * RET_DV] = o * lax.rsqrt(ms + EPS) * (gate * _sigmoid(gate))


def _ret_sample(proj, state, cos, sin):
    b, nq, _ = proj.shape
    inner, q_dec, k_dec, chunk_dec = _ret_tables(math.gcd(nq, RET_CHUNK))
    pad = 128 - nq
    inner_p = jnp.pad(inner, ((0, 0), (0, 0), (0, pad)))
    qd = jnp.broadcast_to(q_dec[:, :, None], (RET_HEADS, nq, RET_DK))
    kd = jnp.broadcast_to(jnp.pad(k_dec, ((0, 0), (0, pad)))[:, :, None], (RET_HEADS, 128, RET_DK))
    cd = jnp.broadcast_to(chunk_dec[:, None, None], (RET_HEADS, RET_DK, RET_DV))
    full = lambda a: pl.BlockSpec(a.shape, lambda bi: (0,) * a.ndim)
    return pl.pallas_call(
        _ret_sample_kernel,
        grid=(b,),
        in_specs=[
            pl.BlockSpec((None, nq, ATT_W), lambda bi: (bi, 0, OFF_QR // ATT_W)),
            pl.BlockSpec((None, nq, ATT_W), lambda bi: (bi, 0, OFF_QR // ATT_W + 1)),
            pl.BlockSpec((None, RET_HEADS, RET_DK, RET_DV), lambda bi: (bi, 0, 0, 0)),
            full(cos), full(sin), full(inner_p), full(qd), full(kd), full(cd),
        ],
        out_specs=[
            pl.BlockSpec((None, nq, RET_V_W), lambda bi: (bi, 0, 0)),
            pl.BlockSpec((None, RET_HEADS, RET_DK, RET_DV), lambda bi: (bi, 0, 0, 0)),
        ],
        out_shape=[jax.ShapeDtypeStruct((b, nq, RET_V_W), F32),
                   jax.ShapeDtypeStruct((b, RET_HEADS, RET_DK, RET_DV), F32)],
        compiler_params=_cparams(("parallel",)),
        name="ret_sample",
    )(proj, proj, state, cos, sin, inner_p, qd, kd, cd)


def _branch_merge_kernel(oa_ref, or_ref, ga_ref, gb_ref, wa_ref, wr_ref, o_ref):
    a = jnp.dot(oa_ref[...], wa_ref[...], preferred_element_type=F32)
    r = jnp.dot(or_ref[...], wr_ref[...], preferred_element_type=F32)
    o_ref[...] = (_sigmoid(ga_ref[...]) * a + _sigmoid(gb_ref[...]) * r).astype(o_ref.dtype)


def _branch_merge(o_att, o_ret, proj, w_att, w_ret, tm, tn):
    t = o_att.shape[0]
    n = w_att.shape[1]
    tm = min(tm, t)
    return pl.pallas_call(
        _branch_merge_kernel,
        grid=(t // tm, n // tn),
        in_specs=[
            pl.BlockSpec((tm, o_att.shape[1]), lambda i, j: (i, 0)),
            pl.BlockSpec((tm, o_ret.shape[1]), lambda i, j: (i, 0)),
            pl.BlockSpec((tm, tn), lambda i, j: (i, OFF_GA // tn + j)),
            pl.BlockSpec((tm, tn), lambda i, j: (i, (OFF_GA + n) // tn + j)),
            pl.BlockSpec((w_att.shape[0], tn), lambda i, j: (0, j)),
            pl.BlockSpec((w_ret.shape[0], tn), lambda i, j: (0, j)),
        ],
        out_specs=pl.BlockSpec((tm, tn), lambda i, j: (i, j)),
        out_shape=jax.ShapeDtypeStruct((t, n), BF16),
        compiler_params=_cparams(("parallel", "arbitrary")),
        name="branch_merge",
    )(o_att, o_ret, proj, proj, w_att, w_ret)


def _residual_matmul_kernel(x_ref, m_ref, w_ref, o_ref):
    o_ref[...] = x_ref[...] + jnp.dot(m_ref[...], w_ref[...], preferred_element_type=F32)


def _residual_matmul(x, mixed, w, tm, tn):
    t, d = x.shape
    tm = min(tm, t)
    return pl.pallas_call(
        _residual_matmul_kernel,
        grid=(t // tm, d // tn),
        in_specs=[
            pl.BlockSpec((tm, tn), lambda i, j: (i, j)),
            pl.BlockSpec((tm, mixed.shape[1]), lambda i, j: (i, 0)),
            pl.BlockSpec((w.shape[0], tn), lambda i, j: (0, j)),
        ],
        out_specs=pl.BlockSpec((tm, tn), lambda i, j: (i, j)),
        out_shape=jax.ShapeDtypeStruct((t, d), F32),
        compiler_params=_cparams(("parallel", "arbitrary")),
        name="mix_out_residual",
    )(x, mixed, w)


def _extract_top(x, n):
    tops = []
    for _ in range(n):
        m = jnp.max(x, axis=0, keepdims=True)
        tops.append(m)
        x = jnp.where(x == m, F32_MIN, x)
    return tops


def _peer_route_kernel(h_ref, nw_ref, wq_ref, sk_ref, hn_ref, s1_ref, s2_ref, e1_ref, e2_ref, tau_ref):
    x = h_ref[...]
    ms = jnp.mean(x * x, axis=-1, keepdims=True)
    hn = (x * lax.rsqrt(ms + EPS) * nw_ref[...]).astype(BF16)
    hn_ref[...] = hn
    q = jnp.dot(hn, wq_ref[...], preferred_element_type=F32).astype(BF16)
    half = PEER_NKEYS
    for h in range(PEER_HEADS):
        tops = []
        for p in range(2):
            hp = 2 * h + p
            s_t = lax.dot_general(sk_ref[hp], q[:, hp * half:(hp + 1) * half], (((1,), (1,)), ((), ())),
                                  preferred_element_type=F32)
            top = _extract_top(s_t, PEER_TOPK)
            tops.append(top)
            e = jnp.exp(s_t - top[0])
            if p == 0:
                s1_ref[h] = s_t
                e1_ref[h] = e
            else:
                s2_ref[h] = s_t
                e2_raw = e
        t2 = jnp.concatenate(tops[1], axis=0)
        cand = jnp.concatenate([tops[0][a] + t2 for a in range(PEER_TOPK)], axis=0)
        best = _extract_top(cand, PEER_TOPK)
        z = jnp.zeros_like(best[0])
        for c in best:
            z = z + jnp.exp(c - best[0])
        e2_ref[h] = e2_raw / z
        tau_ref[pl.ds(h, 1), :] = best[PEER_TOPK - 1]


def _peer_route(h, norm_w, wq, sub_keys, tm):
    t, d = h.shape
    tm = min(tm, t)
    stat = jax.ShapeDtypeStruct((PEER_HEADS, PEER_NKEYS, t), F32)
    stat_spec = pl.BlockSpec((PEER_HEADS, PEER_NKEYS, tm), lambda i: (0, 0, i))
    return pl.pallas_call(
        _peer_route_kernel,
        grid=(t // tm,),
        in_specs=[
            pl.BlockSpec((tm, d), lambda i: (i, 0)),
            pl.BlockSpec((1, d), lambda i: (0, 0)),
            pl.BlockSpec(wq.shape, lambda i: (0, 0)),
            pl.BlockSpec(sub_keys.shape, lambda i: (0, 0, 0)),
        ],
        out_specs=[pl.BlockSpec((tm, d), lambda i: (i, 0))] + [stat_spec] * 4
        + [pl.BlockSpec((PEER_HEADS, tm), lambda i: (0, i))],
        out_shape=[jax.ShapeDtypeStruct((t, d), BF16)] + [stat] * 4
        + [jax.ShapeDtypeStruct((PEER_HEADS, t), F32)],
        compiler_params=_cparams(("parallel",)),
        name="peer_route",
    )(h, norm_w.reshape(1, d), wq, sub_keys)


def _peer_expert_kernel(hn_ref, u_ref, v_ref, s1_ref, e1_ref, s2_ref, e2_ref, tau_ref, h_ref, y_ref,
                        act_s, g_s):
    e_tile = u_ref.shape[0]
    nk = PEER_NKEYS

    @pl.when(pl.program_id(1) == 0)
    def _():
        y_ref[...] = h_ref[...]

    act_s[...] = lax.dot_general(u_ref[...], hn_ref[...], (((1,), (1,)), ((), ())),
                                 preferred_element_type=F32)

    def body(a, carry):
        w = jnp.zeros((nk, hn_ref.shape[0]), F32)
        for h in range(PEER_HEADS):
            cand = s2_ref[h] + s1_ref[h, pl.ds(a, 1), :]
            gate = e2_ref[h] * e1_ref[h, pl.ds(a, 1), :]
            w = w + jnp.where(cand >= tau_ref[pl.ds(h, 1), :], gate, 0.0)
        sl = pl.ds(pl.multiple_of(a * nk, nk), nk)
        g_s[sl, :] = (w * jax.nn.gelu(act_s[sl, :], approximate=True)).astype(BF16)
        return carry

    lax.fori_loop(0, e_tile // nk, body, 0)
    y_ref[...] += lax.dot_general(g_s[...], v_ref[...], (((0,), (0,)), ((), ())),
                                  preferred_element_type=F32)


def _peer_expert(hn, u, v, s1, s2, e1, e2, tau, h, tm, e_tile):
    t, d = h.shape
    tm = min(tm, t)
    n_exp = u.shape[0]
    a_tile = e_tile // PEER_NKEYS
    row_spec = pl.BlockSpec((tm, d), lambda i, e: (i, 0))
    a_spec = pl.BlockSpec((PEER_HEADS, a_tile, tm), lambda i, e: (0, e, i))
    b_spec = pl.BlockSpec((PEER_HEADS, PEER_NKEYS, tm), lambda i, e: (0, 0, i))
    return pl.pallas_call(
        _peer_expert_kernel,
        grid=(t // tm, n_exp // e_tile),
        in_specs=[
            row_spec,
            pl.BlockSpec((e_tile, d), lambda i, e: (e, 0)),
            pl.BlockSpec((e_tile, d), lambda i, e: (e, 0)),
            a_spec, a_spec, b_spec, b_spec,
            pl.BlockSpec((PEER_HEADS, tm), lambda i, e: (0, i)),
            row_spec,
        ],
        out_specs=row_spec,
        out_shape=jax.ShapeDtypeStruct((t, d), F32),
        scratch_shapes=[pltpu.VMEM((e_tile, tm), F32), pltpu.VMEM((e_tile, tm), BF16)],
        compiler_params=_cparams(("parallel", "arbitrary")),
        name="peer_expert",
    )(hn, u, v, s1, e1, s2, e2, tau, h)


def _token_tail(x, proj, o_att, o_ret, w, tm):
    mixed = _branch_merge(o_att, o_ret, proj, w["att_br"], w["ret_br"], tm, 512)
    h = _residual_matmul(x, mixed, w["mix_out"], tm, 512)
    hn, s1, s2, e1, e2, tau = _peer_route(h, w["norm2"], w["peer_q"], w["sub_keys"], 256)
    return _peer_expert(hn, w["peer_u"], w["peer_v"], s1, s2, e1, e2, tau, h, 512, 1024)


def kernel(x_prompt, x_sample, cache_att_w128, cache_att_w512, cache_att_w2048, state_ret, norm1_w, w_in,
           att_q_norm_w, att_k_norm_w, w_att_br, w_ret_br, w_mix_out, norm2_w, peer_w_q, peer_sub_keys,
           peer_u, peer_v):
    depth = w_in.shape[0]
    assert depth == 1, "single-layer trunk"
    b, l, d = x_prompt.shape
    sb, sl, _ = x_sample.shape
    past = cache_att_w2048.shape[2]
    w = {
        "att_br": w_att_br[0].astype(BF16),
        "ret_br": w_ret_br[0].astype(BF16),
        "mix_out": w_mix_out[0].astype(BF16),
        "norm2": norm2_w[0],
        "peer_q": peer_w_q[0].astype(BF16),
        "sub_keys": peer_sub_keys[0].reshape(PEER_HEADS * 2, PEER_NKEYS, -1).astype(BF16),
        "peer_u": peer_u[0].astype(BF16),
        "peer_v": peer_v[0].astype(BF16),
    }
    w_in_b = w_in[0].astype(BF16)
    qn = att_q_norm_w[0].reshape(1, HD)
    kn = att_k_norm_w[0].reshape(1, HD)
    xp = x_prompt.reshape(b * l, d)
    xs = x_sample.reshape(sb * sl, d)

    proj_p = _norm_matmul(xp, norm1_w[0], w_in_b, 1024, 512)
    proj_s = _norm_matmul(xs, norm1_w[0], w_in_b, 1024, 512)
    in_w = proj_p.shape[1]

    cos_p, sin_p = _rope_tables(jnp.arange(l, dtype=jnp.int32), HD)
    kv_p, o_att_p = _attn_prompt(proj_p.reshape(b, l, in_w), cos_p, sin_p, qn, kn)
    o_ret_p, st_p = _ret_prompt(proj_p.reshape(b, l, in_w), cos_p, sin_p)
    y_p = _token_tail(xp, proj_p, o_att_p.reshape(b * l, ATT_OUT), o_ret_p.reshape(b * l, RET_V_W), w, 1024)

    cos_s, sin_s = _rope_tables(past + jnp.arange(sl, dtype=jnp.int32), HD)
    kvw = 2 * ATT_OUT
    c0 = cache_att_w128[0].reshape(sb, -1, kvw)
    c1 = cache_att_w512[0].reshape(sb, -1, kvw)
    c2 = cache_att_w2048[0].reshape(sb, past // 16, 16, kvw)
    kv_s, o_att_s = _attn_sample(proj_s.reshape(sb, sl, in_w), c0, c1, c2, cos_s, sin_s, qn, kn)
    o_ret_s, st_s = _ret_sample(proj_s.reshape(sb, sl, in_w), state_ret[0], cos_s, sin_s)
    y_s = _token_tail(xs, proj_s, o_att_s.reshape(sb * sl, ATT_OUT).astype(BF16),
                      o_ret_s.reshape(sb * sl, RET_V_W).astype(BF16), w, 1024)

    new_p = []
    for g, (win, _) in enumerate(ATT_GROUPS):
        rows = min(win, l)
        kk = kv_p[2 * g][:, l - rows:].reshape(b, rows, 1, ATT_HEADS, HD)
        vv = kv_p[2 * g + 1][:, l - rows:].reshape(b, rows, 1, ATT_HEADS, HD)
        new_p.append(jnp.concatenate([kk, vv], axis=2)[None])
    new_s = [kv.reshape(1, sb, sl, 2, ATT_HEADS, HD) for kv in kv_s]
    return (y_p.reshape(b, l, d), y_s.reshape(sb, sl, d),
            new_p[0], new_p[1], new_p[2], st_p[None],
            new_s[0], new_s[1], new_s[2], st_s[None])
```

```python
import functools
import math

import jax
import jax.numpy as jnp
from jax import lax
from jax.experimental import pallas as pl
from jax.experimental.pallas import tpu as pltpu

F32 = jnp.float32
BF16 = jnp.bfloat16

ATT_GROUPS = ((128, 1), (512, 4), (2048, 16))
N_GROUPS = 3
ATT_HEADS = 8
HD = 128
ATT_W = N_GROUPS * ATT_HEADS * HD
ATT_OUT = ATT_HEADS * HD
N_KEYS = 129
ROPE_THETA = 10000.0
RET_HEADS = 8
RET_DK = 128
RET_DV = 256
RET_QK_W = RET_HEADS * RET_DK
RET_V_W = RET_HEADS * RET_DV
RET_CHUNK = 128
PEER_HEADS = 8
PEER_NKEYS = 128
PEER_TOPK = 16
EPS = 1e-6
NEG = -1e30
F32_MIN = float(jnp.finfo(jnp.float32).min)

OFF_QA = 0
OFF_KA = ATT_W
OFF_VA = 2 * ATT_W
OFF_QR = 3 * ATT_W
OFF_KR = OFF_QR + RET_QK_W
OFF_VR = OFF_KR + RET_QK_W
OFF_GR = OFF_VR + RET_V_W
OFF_GA = OFF_GR + RET_V_W

LANE_CHUNK = 256
VMEM_LIMIT = 56 * 1024 * 1024


def _cparams(sem):
    return pltpu.CompilerParams(dimension_semantics=sem, vmem_limit_bytes=VMEM_LIMIT)


def _sigmoid(x):
    return 1.0 / (1.0 + jnp.exp(-x))


def _rope_tables(pos, width):
    half = width // 2
    inv_freq = jnp.power(jnp.float32(ROPE_THETA), -jnp.arange(half, dtype=F32) / half)
    ang = pos.astype(F32)[:, None] * inv_freq[None, :]
    cos, sin = jnp.cos(ang), jnp.sin(ang)
    return jnp.concatenate([cos, cos], axis=-1), jnp.concatenate([-sin, sin], axis=-1)


def _rope(x, cos, sin):
    return x * cos + pltpu.roll(x, HD // 2, 1) * sin


def _head_rmsnorm(x, w):
    ms = jnp.mean(x * x, axis=-1, keepdims=True)
    return x * lax.rsqrt(ms + EPS) * w


def _norm_matmul_kernel(x_ref, nw_ref, w_ref, o_ref, xn_ref):
    @pl.when(pl.program_id(1) == 0)
    def _():
        x = x_ref[...]
        ms = jnp.mean(x * x, axis=-1, keepdims=True)
        xn_ref[...] = (x * lax.rsqrt(ms + EPS) * nw_ref[...]).astype(BF16)

    o_ref[...] = jnp.dot(xn_ref[...], w_ref[...], preferred_element_type=F32)


def _norm_matmul(x, norm_w, w_bf16, tm, tn):
    t, d = x.shape
    n = w_bf16.shape[1]
    tm = min(tm, t)
    return pl.pallas_call(
        _norm_matmul_kernel,
        grid=(t // tm, n // tn),
        in_specs=[
            pl.BlockSpec((tm, d), lambda i, j: (i, 0)),
            pl.BlockSpec((1, d), lambda i, j: (0, 0)),
            pl.BlockSpec((d, tn), lambda i, j: (0, j)),
        ],
        out_specs=pl.BlockSpec((tm, tn), lambda i, j: (i, j)),
        out_shape=jax.ShapeDtypeStruct((t, n), F32),
        scratch_shapes=[pltpu.VMEM((tm, d), BF16)],
        compiler_params=_cparams(("parallel", "arbitrary")),
        name="norm_inproj",
    )(x, norm_w.reshape(1, d), w_bf16)


def _attn_prompt_kernel(q0, k0, v0, q1, k1, v1, q2, k2, v2, cos_ref, sin_ref, qn_ref, kn_ref,
                        ko0, vo0, ko1, vo1, ko2, vo2, o_ref,
                        q_s, k_s, v_s, og_s, lse_s):
    L = q0.shape[0]
    blk = 128
    scale = HD ** -0.5
    cos = cos_ref[...]
    sin = sin_ref[...]
    k_s[pl.ds(0, L), :] = jnp.zeros((L, HD), F32)
    v_s[pl.ds(0, L), :] = jnp.zeros((L, HD), F32)
    groups = ((q0, k0, v0, ko0, vo0), (q1, k1, v1, ko1, vo1), (q2, k2, v2, ko2, vo2))
    for g, (qr, kr, vr, ko, vo) in enumerate(groups):
        win, dil = ATT_GROUPS[g]
        q = _rope(_head_rmsnorm(qr[...], qn_ref[...]), cos, sin) * scale
        k = _rope(_head_rmsnorm(kr[...], kn_ref[...]), cos, sin)
        v = vr[...]
        ko[...] = k
        vo[...] = v
        q_s[...] = q
        k_s[pl.ds(L, L), :] = k
        v_s[pl.ds(L, L), :] = v
        log_d = int(math.log2(dil))

        def body(t, carry, dil=dil, log_d=log_d, g=g):
            r = jnp.bitwise_and(t, dil - 1)
            i = lax.shift_right_logical(t, log_d)
            qstart = r + dil * blk * i
            kstart = L + qstart - dil * blk
            if dil == 1:
                qsl = pl.ds(qstart, blk)
                ksl = pl.ds(kstart, 2 * blk)
            else:
                qsl = pl.ds(qstart, blk, stride=dil)
                ksl = pl.ds(kstart, 2 * blk, stride=dil)
            qb = q_s[qsl, :].astype(BF16)
            kb = k_s[ksl, :].astype(BF16)
            vb = v_s[ksl, :].astype(BF16)
            s = lax.dot_general(qb, kb, (((1,), (1,)), ((), ())), preferred_element_type=F32)
            qi = lax.broadcasted_iota(jnp.int32, (blk, 2 * blk), 0)
            kj = lax.broadcasted_iota(jnp.int32, (blk, 2 * blk), 1)
            lo = jnp.maximum(qi, jnp.where(i == 0, blk, 0))
            valid = jnp.logical_and(kj >= lo, kj <= qi + blk)
            s = jnp.where(valid, s, NEG)
            m = jnp.max(s, axis=-1, keepdims=True)
            p = jnp.exp(s - m)
            den = jnp.sum(p, axis=-1, keepdims=True)
            o = jnp.dot((p / den).astype(BF16), vb, preferred_element_type=F32)
            lse = m + jnp.log(den)
            osl = pl.ds(g * L + qstart, blk) if dil == 1 else pl.ds(g * L + qstart, blk, stride=dil)
            og_s[osl, :] = o
            lse_s[osl, :] = jnp.broadcast_to(lse, (blk, HD))
            return carry

        lax.fori_loop(0, L // blk, body, 0, unroll=True)

    l0 = lse_s[pl.ds(0, L), :]
    l1 = lse_s[pl.ds(L, L), :]
    l2 = lse_s[pl.ds(2 * L, L), :]
    mx = jnp.maximum(jnp.maximum(l0, l1), l2)
    e0 = jnp.exp(l0 - mx)
    e1 = jnp.exp(l1 - mx)
    e2 = jnp.exp(l2 - mx)
    tot = e0 + e1 + e2
    o = ((e0 / tot) * og_s[pl.ds(0, L), :] + (e1 / tot) * og_s[pl.ds(L, L), :]
         + (e2 / tot) * og_s[pl.ds(2 * L, L), :])
    o_ref[...] = o.astype(o_ref.dtype)


def _attn_prompt(proj, cos, sin, qn, kn):
    b, l, _ = proj.shape
    in_specs = []
    for g in range(N_GROUPS):
        for off in (OFF_QA, OFF_KA, OFF_VA):
            cb = (off + g * ATT_OUT) // HD
            in_specs.append(pl.BlockSpec((None, l, HD), lambda bi, hi, cb=cb: (bi, 0, cb + hi)))
    in_specs += [
        pl.BlockSpec((l, HD), lambda bi, hi: (0, 0)),
        pl.BlockSpec((l, HD), lambda bi, hi: (0, 0)),
        pl.BlockSpec((1, HD), lambda bi, hi: (0, 0)),
        pl.BlockSpec((1, HD), lambda bi, hi: (0, 0)),
    ]
    head_spec = pl.BlockSpec((None, l, HD), lambda bi, hi: (bi, 0, hi))
    kv_shape = jax.ShapeDtypeStruct((b, l, ATT_OUT), F32)
    outs = pl.pallas_call(
        _attn_prompt_kernel,
        grid=(b, ATT_HEADS),
        in_specs=in_specs,
        out_specs=[head_spec] * 7,
        out_shape=[kv_shape] * 6 + [jax.ShapeDtypeStruct((b, l, ATT_OUT), BF16)],
        scratch_shapes=[
            pltpu.VMEM((l, HD), F32),
            pltpu.VMEM((2 * l, HD), F32),
            pltpu.VMEM((2 * l, HD), F32),
            pltpu.VMEM((N_GROUPS * l, HD), F32),
            pltpu.VMEM((N_GROUPS * l, HD), F32),
        ],
        compiler_params=_cparams(("parallel", "arbitrary")),
        name="attn_prompt",
    )(*([proj] * 9), cos, sin, qn, kn)
    return outs[:6], outs[6]


def _attn_sample_kernel(qa_ref, ka_ref, va_ref, c0_ref, c1_ref, c2_ref, cos_ref, sin_ref,
                        qn_ref, kn_ref, kv0_ref, kv1_ref, kv2_ref, o_ref):
    nq = qa_ref.shape[0]
    scale = HD ** -0.5
    cos = cos_ref[...]
    sin = sin_ref[...]
    qn = qn_ref[...]
    kn = kn_ref[...]
    rows = ATT_HEADS * nq
    bd_r = lax.broadcasted_iota(jnp.int32, (rows, ATT_OUT), 0)
    bd_c = lax.broadcasted_iota(jnp.int32, (rows, ATT_OUT), 1)
    blockdiag = (lax.shift_right_logical(bd_r, int(math.log2(nq)))
                 == lax.shift_right_logical(bd_c, int(math.log2(HD))))
    caches = (c0_ref, c1_ref, c2_ref)
    kv_refs = (kv0_ref, kv1_ref, kv2_ref)
    o_groups = []
    lse_groups = []
    for g in range(N_GROUPS):
        win, dil = ATT_GROUPS[g]
        qs, ks, vs = [], [], []
        for h in range(ATT_HEADS):
            c0 = g * ATT_OUT + h * HD
            qs.append(_rope(_head_rmsnorm(qa_ref[:, c0:c0 + HD], qn), cos, sin) * scale)
            ks.append(_rope(_head_rmsnorm(ka_ref[:, c0:c0 + HD], kn), cos, sin))
        q_g = jnp.concatenate(qs, axis=1)
        k_g = jnp.concatenate(ks, axis=1)
        v_g = va_ref[:, g * ATT_OUT:(g + 1) * ATT_OUT]
        kv_refs[g][:, 0:ATT_OUT] = k_g
        kv_refs[g][:, ATT_OUT:2 * ATT_OUT] = v_g
        c_ref = caches[g]
        wp = c_ref.shape[0]
        ncol = wp + 128
        zpad = jnp.zeros((128 - nq, ATT_OUT), F32)
        k_new = jnp.concatenate([k_g, zpad], axis=0).astype(BF16)
        v_new = jnp.concatenate([v_g, zpad], axis=0).astype(BF16)
        qbd = jnp.where(blockdiag, jnp.concatenate([q_g] * ATT_HEADS, axis=0), 0.0).astype(BF16)
        nt = (((1,), (1,)), ((), ()))
        s = jnp.concatenate(
            [lax.dot_general(qbd, c_ref[:, 0:ATT_OUT], nt, preferred_element_type=F32),
             lax.dot_general(qbd, k_new, nt, preferred_element_type=F32)], axis=1)
        qi = jnp.bitwise_and(lax.broadcasted_iota(jnp.int32, (rows, ncol), 0), nq - 1)
        c = lax.broadcasted_iota(jnp.int32, (rows, ncol), 1)
        if g == 2:
            valid_cache = jnp.bitwise_and(c, 7) == qi
        else:
            valid_cache = jnp.logical_and(c >= qi, jnp.bitwise_and(c - qi, dil - 1) == 0)
        t = c - wp
        valid_new = jnp.logical_and(t <= qi, jnp.bitwise_and(qi - t, dil - 1) == 0)
        valid = jnp.logical_or(jnp.logical_and(c < wp, valid_cache),
                               jnp.logical_and(c >= wp, valid_new))
        s = jnp.where(valid, s, NEG)
        m = jnp.max(s, axis=-1, keepdims=True)
        p = jnp.exp(s - m)
        den = jnp.sum(p, axis=-1, keepdims=True)
        pn = (p / den).astype(BF16)
        o = (jnp.dot(pn[:, 0:wp], c_ref[:, ATT_OUT:2 * ATT_OUT], preferred_element_type=F32)
             + jnp.dot(pn[:, wp:ncol], v_new, preferred_element_type=F32))
        lse = m + jnp.log(den)
        o_groups.append([o[h * nq:(h + 1) * nq, h * HD:(h + 1) * HD] for h in range(ATT_HEADS)])
        lse_groups.append([lse[h * nq:(h + 1) * nq, :] for h in range(ATT_HEADS)])
    outs = []
    for h in range(ATT_HEADS):
        l0, l1, l2 = lse_groups[0][h], lse_groups[1][h], lse_groups[2][h]
        mx = jnp.maximum(jnp.maximum(l0, l1), l2)
        e0, e1, e2 = jnp.exp(l0 - mx), jnp.exp(l1 - mx), jnp.exp(l2 - mx)
        tot = e0 + e1 + e2
        outs.append((e0 / tot) * o_groups[0][h] + (e1 / tot) * o_groups[1][h]
                    + (e2 / tot) * o_groups[2][h])
    o_ref[...] = jnp.concatenate(outs, axis=1)


def _attn_sample(proj, c0, c1, c2, cos, sin, qn, kn):
    b, nq, _ = proj.shape
    kvw = 2 * ATT_OUT
    small = lambda shape: pl.BlockSpec(shape, lambda bi: (0, 0))
    outs = pl.pallas_call(
        _attn_sample_kernel,
        grid=(b,),
        in_specs=[
            pl.BlockSpec((None, nq, ATT_W), lambda bi: (bi, 0, OFF_QA // ATT_W)),
            pl.BlockSpec((None, nq, ATT_W), lambda bi: (bi, 0, OFF_KA // ATT_W)),
            pl.BlockSpec((None, nq, ATT_W), lambda bi: (bi, 0, OFF_VA // ATT_W)),
            pl.BlockSpec((None, c0.shape[1], kvw), lambda bi: (bi, 0, 0)),
            pl.BlockSpec((None, c1.shape[1], kvw), lambda bi: (bi, 0, 0)),
            pl.BlockSpec((None, c2.shape[1], kvw), lambda bi: (bi, 0, 0)),
            small((nq, HD)), small((nq, HD)), small((1, HD)), small((1, HD)),
        ],
        out_specs=[pl.BlockSpec((None, nq, kvw), lambda bi: (bi, 0, 0))] * 3
        + [pl.BlockSpec((None, nq, ATT_OUT), lambda bi: (bi, 0, 0))],
        out_shape=[jax.ShapeDtypeStruct((b, nq, kvw), F32)] * 3
        + [jax.ShapeDtypeStruct((b, nq, ATT_OUT), F32)],
        compiler_params=_cparams(("parallel",)),
        name="attn_sample",
    )(proj, proj, proj, c0, c1, c2, cos, sin, qn, kn)
    return outs[:3], outs[3]


def _ret_tables(c):
    log_g = jnp.log1p(-jnp.exp2(-5.0 - jnp.arange(RET_HEADS, dtype=F32)))
    ii = jnp.arange(c, dtype=F32)
    diff = ii[:, None] - ii[None, :]
    inner = jnp.where(diff[None] >= 0, jnp.exp(jnp.maximum(diff, 0.0)[None] * log_g[:, None, None]), 0.0)
    q_dec = jnp.exp((ii + 1.0)[None, :] * log_g[:, None])
    k_dec = jnp.exp((c - 1.0 - ii)[None, :] * log_g[:, None])
    chunk_dec = jnp.exp(c * log_g)
    return inner, q_dec, k_dec, chunk_dec


def _ret_prompt_kernel(q_ref, k_ref, v_ref, g_ref, cos_ref, sin_ref, inner_ref, qd_ref, kd_ref, cd_ref,
                       o_ref, st_ref, q_s, k_s, s_s):
    L = q_ref.shape[0]
    C = RET_CHUNK
    q_s[...] = _rope(q_ref[...], cos_ref[...], sin_ref[...])
    k_s[...] = _rope(k_ref[...], cos_ref[...], sin_ref[...]) * (RET_DK ** -0.5)
    s_s[...] = jnp.zeros(s_s.shape, F32)
    inner = inner_ref[...]
    qd = qd_ref[...]
    kd = kd_ref[...]
    cd = cd_ref[...]

    def body(ci, carry):
        sl = pl.ds(pl.multiple_of(ci * C, C), C)
        qc = q_s[sl, :]
        kc = k_s[sl, :]
        vc = v_ref[sl, :].astype(BF16)
        state = s_s[...]
        sc = lax.dot_general(qc.astype(BF16), kc.astype(BF16), (((1,), (1,)), ((), ())),
                             preferred_element_type=F32) * inner
        o = (jnp.dot(sc.astype(BF16), vc, preferred_element_type=F32)
             + jnp.dot((qc * qd).astype(BF16), state.astype(BF16), preferred_element_type=F32))
        kdt = jnp.transpose(kc * kd).astype(BF16)
        s_s[...] = state * cd + jnp.dot(kdt, vc, preferred_element_type=F32)
        ms = jnp.mean(o * o, axis=-1, keepdims=True)
        gate = g_ref[sl, :]
        o_ref[sl, :] = (o * lax.rsqrt(ms + EPS) * (gate * _sigmoid(gate))).astype(o_ref.dtype)
        return carry

    lax.fori_loop(0, L // C, body, 0)
    st_ref[...] = s_s[...]


def _ret_prompt(proj, cos, sin):
    b, l, _ = proj.shape
    inner, q_dec, k_dec, chunk_dec = _ret_tables(RET_CHUNK)
    qd = jnp.broadcast_to(q_dec[:, :, None], (RET_HEADS, RET_CHUNK, RET_DK))
    kd = jnp.broadcast_to(k_dec[:, :, None], (RET_HEADS, RET_CHUNK, RET_DK))
    cd = jnp.broadcast_to(chunk_dec[:, None, None], (RET_HEADS, RET_DK, RET_DV))
    per_head = lambda r, c: pl.BlockSpec((None, r, c), lambda bi, hi: (hi, 0, 0))
    return pl.pallas_call(
        _ret_prompt_kernel,
        grid=(b, RET_HEADS),
        in_specs=[
            pl.BlockSpec((None, l, RET_DK), lambda bi, hi: (bi, 0, OFF_QR // RET_DK + hi)),
            pl.BlockSpec((None, l, RET_DK), lambda bi, hi: (bi, 0, OFF_KR // RET_DK + hi)),
            pl.BlockSpec((None, l, RET_DV), lambda bi, hi: (bi, 0, OFF_VR // RET_DV + hi)),
            pl.BlockSpec((None, l, RET_DV), lambda bi, hi: (bi, 0, OFF_GR // RET_DV + hi)),
            pl.BlockSpec((l, RET_DK), lambda bi, hi: (0, 0)),
            pl.BlockSpec((l, RET_DK), lambda bi, hi: (0, 0)),
            per_head(RET_CHUNK, RET_CHUNK), per_head(RET_CHUNK, RET_DK), per_head(RET_CHUNK, RET_DK),
            per_head(RET_DK, RET_DV),
        ],
        out_specs=[
            pl.BlockSpec((None, l, RET_DV), lambda bi, hi: (bi, 0, hi)),
            pl.BlockSpec((None, None, RET_DK, RET_DV), lambda bi, hi: (bi, hi, 0, 0)),
        ],
        out_shape=[jax.ShapeDtypeStruct((b, l, RET_V_W), BF16),
                   jax.ShapeDtypeStruct((b, RET_HEADS, RET_DK, RET_DV), F32)],
        scratch_shapes=[pltpu.VMEM((l, RET_DK), F32), pltpu.VMEM((l, RET_DK), F32),
                        pltpu.VMEM((RET_DK, RET_DV), F32)],
        compiler_params=_cparams(("parallel", "arbitrary")),
        name="ret_prompt",
    )(proj, proj, proj, proj, cos, sin, inner, qd, kd, cd)


def _ret_sample_kernel(a_ref, b_ref, s0_ref, cos_ref, sin_ref, inner_ref, qd_ref, kd_ref, cd_ref,
                       o_ref, st_ref):
    nq = a_ref.shape[0]
    cos = cos_ref[...]
    sin = sin_ref[...]
    pad = 128 - nq
    for h in range(RET_HEADS):
        q = _rope(a_ref[:, h * RET_DK:(h + 1) * RET_DK], cos, sin)
        k = _rope(a_ref[:, RET_QK_W + h * RET_DK:RET_QK_W + (h + 1) * RET_DK], cos, sin) * (RET_DK ** -0.5)
        vc0 = h * RET_DV
        if vc0 < ATT_OUT:
            v = a_ref[:, 2 * RET_QK_W + vc0:2 * RET_QK_W + vc0 + RET_DV]
        else:
            v = b_ref[:, vc0 - ATT_OUT:vc0 - ATT_OUT + RET_DV]
        gate = b_ref[:, ATT_OUT + h * RET_DV:ATT_OUT + (h + 1) * RET_DV]
        state = s0_ref[h]
        kp = jnp.concatenate([k, jnp.zeros((pad, RET_DK), F32)], axis=0)
        vp = jnp.concatenate([v, jnp.zeros((pad, RET_DV), F32)], axis=0).astype(BF16)
        sc = lax.dot_general(q.astype(BF16), kp.astype(BF16), (((1,), (1,)), ((), ())),
                             preferred_element_type=F32) * inner_ref[h]
        o = (jnp.dot(sc.astype(BF16), vp, preferred_element_type=F32)
             + jnp.dot((q * qd_ref[h]).astype(BF16), state.astype(BF16), preferred_element_type=F32))
        kdt = jnp.transpose(kp * kd_ref[h]).astype(BF16)
        st_ref[h] = state * cd_ref[h] + jnp.dot(kdt, vp, preferred_element_type=F32)
        ms = jnp.mean(o * o, axis=-1, keepdims=True)
        o_ref[:, h * RET_DV:(h + 1) * RET_DV] = o * lax.rsqrt(ms + EPS) * (gate * _sigmoid(gate))


def _ret_sample(proj, state, cos, sin):
    b, nq, _ = proj.shape
    inner, q_dec, k_dec, chunk_dec = _ret_tables(math.gcd(nq, RET_CHUNK))
    pad = 128 - nq
    inner_p = jnp.pad(inner, ((0, 0), (0, 0), (0, pad)))
    qd = jnp.broadcast_to(q_dec[:, :, None], (RET_HEADS, nq, RET_DK))
    kd = jnp.broadcast_to(jnp.pad(k_dec, ((0, 0), (0, pad)))[:, :, None], (RET_HEADS, 128, RET_DK))
    cd = jnp.broadcast_to(chunk_dec[:, None, None], (RET_HEADS, RET_DK, RET_DV))
    full = lambda a: pl.BlockSpec(a.shape, lambda bi: (0,) * a.ndim)
    return pl.pallas_call(
        _ret_sample_kernel,
        grid=(b,),
        in_specs=[
            pl.BlockSpec((None, nq, ATT_W), lambda bi: (bi, 0, OFF_QR // ATT_W)),
            pl.BlockSpec((None, nq, ATT_W), lambda bi: (bi, 0, OFF_QR // ATT_W + 1)),
            pl.BlockSpec((None, RET_HEADS, RET_DK, RET_DV), lambda bi: (bi, 0, 0, 0)),
            full(cos), full(sin), full(inner_p), full(qd), full(kd), full(cd),
        ],
        out_specs=[
            pl.BlockSpec((None, nq, RET_V_W), lambda bi: (bi, 0, 0)),
            pl.BlockSpec((None, RET_HEADS, RET_DK, RET_DV), lambda bi: (bi, 0, 0, 0)),
        ],
        out_shape=[jax.ShapeDtypeStruct((b, nq, RET_V_W), F32),
                   jax.ShapeDtypeStruct((b, RET_HEADS, RET_DK, RET_DV), F32)],
        compiler_params=_cparams(("parallel",)),
        name="ret_sample",
    )(proj, proj, state, cos, sin, inner_p, qd, kd, cd)


def _branch_merge_kernel(oa_ref, or_ref, ga_ref, gb_ref, wa_ref, wr_ref, o_ref):
    a = jnp.dot(oa_ref[...], wa_ref[...], preferred_element_type=F32)
    r = jnp.dot(or_ref[...], wr_ref[...], preferred_element_type=F32)
    o_ref[...] = (_sigmoid(ga_ref[...]) * a + _sigmoid(gb_ref[...]) * r).astype(o_ref.dtype)


def _branch_merge(o_att, o_ret, proj, w_att, w_ret, tm, tn):
    t = o_att.shape[0]
    n = w_att.shape[1]
    tm = min(tm, t)
    return pl.pallas_call(
        _branch_merge_kernel,
        grid=(t // tm, n // tn),
        in_specs=[
            pl.BlockSpec((tm, o_att.shape[1]), lambda i, j: (i, 0)),
            pl.BlockSpec((tm, o_ret.shape[1]), lambda i, j: (i, 0)),
            pl.BlockSpec((tm, tn), lambda i, j: (i, OFF_GA // tn + j)),
            pl.BlockSpec((tm, tn), lambda i, j: (i, (OFF_GA + n) // tn + j)),
            pl.BlockSpec((w_att.shape[0], tn), lambda i, j: (0, j)),
            pl.BlockSpec((w_ret.shape[0], tn), lambda i, j: (0, j)),
        ],
        out_specs=pl.BlockSpec((tm, tn), lambda i, j: (i, j)),
        out_shape=jax.ShapeDtypeStruct((t, n), BF16),
        compiler_params=_cparams(("parallel", "arbitrary")),
        name="branch_merge",
    )(o_att, o_ret, proj, proj, w_att, w_ret)


def _residual_matmul_kernel(x_ref, m_ref, w_ref, o_ref):
    o_ref[...] = x_ref[...] + jnp.dot(m_ref[...], w_ref[...], preferred_element_type=F32)


def _residual_matmul(x, mixed, w, tm, tn):
    t, d = x.shape
    tm = min(tm, t)
    return pl.pallas_call(
        _residual_matmul_kernel,
        grid=(t // tm, d // tn),
        in_specs=[
            pl.BlockSpec((tm, tn), lambda i, j: (i, j)),
            pl.BlockSpec((tm, mixed.shape[1]), lambda i, j: (i, 0)),
            pl.BlockSpec((w.shape[0], tn), lambda i, j: (0, j)),
        ],
        out_specs=pl.BlockSpec((tm, tn), lambda i, j: (i, j)),
        out_shape=jax.ShapeDtypeStruct((t, d), F32),
        compiler_params=_cparams(("parallel", "arbitrary")),
        name="mix_out_residual",
    )(x, mixed, w)


def _extract_top(x, n):
    tops = []
    for _ in range(n):
        m = jnp.max(x, axis=0, keepdims=True)
        tops.append(m)
        x = jnp.where(x == m, F32_MIN, x)
    return tops


def _peer_route_kernel(h_ref, nw_ref, wq_ref, sk_ref, hn_ref, s1_ref, s2_ref, e1_ref, e2_ref, tau_ref):
    x = h_ref[...]
    ms = jnp.mean(x * x, axis=-1, keepdims=True)
    hn = (x * lax.rsqrt(ms + EPS) * nw_ref[...]).astype(BF16)
    hn_ref[...] = hn
    q = jnp.dot(hn, wq_ref[...], preferred_element_type=F32).astype(BF16)
    half = PEER_NKEYS
    for h in range(PEER_HEADS):
        tops = []
        for p in range(2):
            hp = 2 * h + p
            s_t = lax.dot_general(sk_ref[hp], q[:, hp * half:(hp + 1) * half], (((1,), (1,)), ((), ())),
                                  preferred_element_type=F32)
            top = _extract_top(s_t, PEER_TOPK)
            tops.append(top)
            e = jnp.exp(s_t - top[0])
            if p == 0:
                s1_ref[h] = s_t
                e1_ref[h] = e
            else:
                s2_ref[h] = s_t
                e2_raw = e
        t2 = jnp.concatenate(tops[1], axis=0)
        cand = jnp.concatenate([tops[0][a] + t2 for a in range(PEER_TOPK)], axis=0)
        best = _extract_top(cand, PEER_TOPK + 1)
        z = jnp.zeros_like(best[0])
        for c in best[:PEER_TOPK]:
            z = z + jnp.exp(c - best[0])
        e2_ref[h] = e2_raw / z
        tau_ref[pl.ds(h, 1), :] = 0.5 * (best[PEER_TOPK - 1] + best[PEER_TOPK])


def _peer_route(h, norm_w, wq, sub_keys, tm):
    t, d = h.shape
    tm = min(tm, t)
    stat = jax.ShapeDtypeStruct((PEER_HEADS, PEER_NKEYS, t), F32)
    stat_spec = pl.BlockSpec((PEER_HEADS, PEER_NKEYS, tm), lambda i: (0, 0, i))
    return pl.pallas_call(
        _peer_route_kernel,
        grid=(t // tm,),
        in_specs=[
            pl.BlockSpec((tm, d), lambda i: (i, 0)),
            pl.BlockSpec((1, d), lambda i: (0, 0)),
            pl.BlockSpec(wq.shape, lambda i: (0, 0)),
            pl.BlockSpec(sub_keys.shape, lambda i: (0, 0, 0)),
        ],
        out_specs=[pl.BlockSpec((tm, d), lambda i: (i, 0))] + [stat_spec] * 4
        + [pl.BlockSpec((PEER_HEADS, tm), lambda i: (0, i))],
        out_shape=[jax.ShapeDtypeStruct((t, d), BF16)] + [stat] * 4
        + [jax.ShapeDtypeStruct((PEER_HEADS, t), F32)],
        compiler_params=_cparams(("parallel",)),
        name="peer_route",
    )(h, norm_w.reshape(1, d), wq, sub_keys)


def _gelu_tanh(x):
    c = math.sqrt(2.0 / math.pi)
    inner = x * (c + (c * 0.044715) * (x * x))
    half = 0.5 * x
    return half + half * jnp.tanh(inner)


def _peer_expert_kernel(hn_ref, u_ref, v_ref, s1_ref, e1_ref, s2_ref, e2_ref, tau_ref, h_ref, y_ref,
                        act_s, g_s):
    e_tile = u_ref.shape[0]
    tm = hn_ref.shape[0]
    nk = PEER_NKEYS
    lc = min(LANE_CHUNK, tm)

    @pl.when(pl.program_id(1) == 0)
    def _():
        y_ref[...] = h_ref[...]

    act_s[...] = lax.dot_general(u_ref[...], hn_ref[...], (((1,), (1,)), ((), ())),
                                 preferred_element_type=F32)

    def body(a, carry):
        rows = pl.ds(pl.multiple_of(a * nk, nk), nk)
        thr = [tau_ref[h:h + 1, :] - s1_ref[h, pl.ds(a, 1), :] for h in range(PEER_HEADS)]
        e1rows = [e1_ref[h, pl.ds(a, 1), :] for h in range(PEER_HEADS)]
        for tcol in range(tm // lc):
            lanes = slice(tcol * lc, (tcol + 1) * lc)
            w = jnp.zeros((nk, lc), F32)
            for h in range(PEER_HEADS):
                gate = e2_ref[h, :, lanes] * e1rows[h][:, lanes]
                w = w + jnp.where(s2_ref[h, :, lanes] >= thr[h][:, lanes], gate, 0.0)
            g_s[rows, lanes] = (w * _gelu_tanh(act_s[rows, lanes])).astype(BF16)
        return carry

    lax.fori_loop(0, e_tile // nk, body, 0)
    y_ref[...] += lax.dot_general(g_s[...], v_ref[...], (((0,), (0,)), ((), ())),
                                  preferred_element_type=F32)


def _peer_expert(hn, u, v, s1, s2, e1, e2, tau, h, tm, e_tile):
    t, d = h.shape
    tm = min(tm, t)
    n_exp = u.shape[0]
    a_tile = e_tile // PEER_NKEYS
    row_spec = pl.BlockSpec((tm, d), lambda i, e: (i, 0))
    a_spec = pl.BlockSpec((PEER_HEADS, a_tile, tm), lambda i, e: (0, e, i))
    b_spec = pl.BlockSpec((PEER_HEADS, PEER_NKEYS, tm), lambda i, e: (0, 0, i))
    return pl.pallas_call(
        _peer_expert_kernel,
        grid=(t // tm, n_exp // e_tile),
        in_specs=[
            row_spec,
            pl.BlockSpec((e_tile, d), lambda i, e: (e, 0)),
            pl.BlockSpec((e_tile, d), lambda i, e: (e, 0)),
            a_spec, a_spec, b_spec, b_spec,
            pl.BlockSpec((PEER_HEADS, tm), lambda i, e: (0, i)),
            row_spec,
        ],
        out_specs=row_spec,
        out_shape=jax.ShapeDtypeStruct((t, d), F32),
        scratch_shapes=[pltpu.VMEM((e_tile, tm), F32), pltpu.VMEM((e_tile, tm), BF16)],
        compiler_params=_cparams(("parallel", "arbitrary")),
        name="peer_expert",
    )(hn, u, v, s1, e1, s2, e2, tau, h)


def _token_tail(x, proj, o_att, o_ret, w, tm):
    mixed = _branch_merge(o_att, o_ret, proj, w["att_br"], w["ret_br"], tm, 512)
    h = _residual_matmul(x, mixed, w["mix_out"], tm, 512)
    hn, s1, s2, e1, e2, tau = _peer_route(h, w["norm2"], w["peer_q"], w["sub_keys"], 256)
    return _peer_expert(hn, w["peer_u"], w["peer_v"], s1, s2, e1, e2, tau, h, 512, 1024)


def kernel(x_prompt, x_sample, cache_att_w128, cache_att_w512, cache_att_w2048, state_ret, norm1_w, w_in,
           att_q_norm_w, att_k_norm_w, w_att_br, w_ret_br, w_mix_out, norm2_w, peer_w_q, peer_sub_keys,
           peer_u, peer_v):
    depth = w_in.shape[0]
    assert depth == 1, "single-layer trunk"
    b, l, d = x_prompt.shape
    sb, sl, _ = x_sample.shape
    past = cache_att_w2048.shape[2]
    w = {
        "att_br": w_att_br[0].astype(BF16),
        "ret_br": w_ret_br[0].astype(BF16),
        "mix_out": w_mix_out[0].astype(BF16),
        "norm2": norm2_w[0],
        "peer_q": peer_w_q[0].astype(BF16),
        "sub_keys": peer_sub_keys[0].reshape(PEER_HEADS * 2, PEER_NKEYS, -1).astype(BF16),
        "peer_u": peer_u[0].astype(BF16),
        "peer_v": peer_v[0].astype(BF16),
    }
    w_in_b = w_in[0].astype(BF16)
    qn = att_q_norm_w[0].reshape(1, HD)
    kn = att_k_norm_w[0].reshape(1, HD)
    xp = x_prompt.reshape(b * l, d)
    xs = x_sample.reshape(sb * sl, d)

    proj_p = _norm_matmul(xp, norm1_w[0], w_in_b, 1024, 512)
    proj_s = _norm_matmul(xs, norm1_w[0], w_in_b, 1024, 512)
    in_w = proj_p.shape[1]

    cos_p, sin_p = _rope_tables(jnp.arange(l, dtype=jnp.int32), HD)
    kv_p, o_att_p = _attn_prompt(proj_p.reshape(b, l, in_w), cos_p, sin_p, qn, kn)
    o_ret_p, st_p = _ret_prompt(proj_p.reshape(b, l, in_w), cos_p, sin_p)
    y_p = _token_tail(xp, proj_p, o_att_p.reshape(b * l, ATT_OUT), o_ret_p.reshape(b * l, RET_V_W), w, 1024)

    cos_s, sin_s = _rope_tables(past + jnp.arange(sl, dtype=jnp.int32), HD)
    kvw = 2 * ATT_OUT
    c0 = cache_att_w128[0].reshape(sb, -1, kvw).astype(BF16)
    c1 = cache_att_w512[0].reshape(sb, -1, kvw).astype(BF16)
    c2 = cache_att_w2048[0].reshape(sb, past // 16, 16, kvw)[:, :, :sl].reshape(sb, -1, kvw).astype(BF16)
    kv_s, o_att_s = _attn_sample(proj_s.reshape(sb, sl, in_w), c0, c1, c2, cos_s, sin_s, qn, kn)
    o_ret_s, st_s = _ret_sample(proj_s.reshape(sb, sl, in_w), state_ret[0], cos_s, sin_s)
    y_s = _token_tail(xs, proj_s, o_att_s.reshape(sb * sl, ATT_OUT).astype(BF16),
                      o_ret_s.reshape(sb * sl, RET_V_W).astype(BF16), w, 1024)

    new_p = []
    for g, (win, _) in enumerate(ATT_GROUPS):
        rows = min(win, l)
        kk = kv_p[2 * g][:, l - rows:].reshape(b, rows, 1, ATT_HEADS, HD)
        vv = kv_p[2 * g + 1][:, l - rows:].reshape(b, rows, 1, ATT_HEADS, HD)
        new_p.append(jnp.concatenate([kk, vv], axis=2)[None])
    new_s = [kv.reshape(1, sb, sl, 2, ATT_HEADS, HD) for kv in kv_s]
    return (y_p.reshape(b, l, d), y_s.reshape(sb, sl, d),
            new_p[0], new_p[1], new_p[2], st_p[None],
            new_s[0], new_s[1], new_s[2], st_s[None])
```

```python
import functools
import math

import jax
import jax.numpy as jnp
from jax import lax
from jax.experimental import pallas as pl
from jax.experimental.pallas import tpu as pltpu

F32 = jnp.float32
BF16 = jnp.bfloat16

ATT_GROUPS = ((128, 1), (512, 4), (2048, 16))
N_GROUPS = 3
ATT_HEADS = 8
HD = 128
ATT_W = N_GROUPS * ATT_HEADS * HD
ATT_OUT = ATT_HEADS * HD
N_KEYS = 129
ROPE_THETA = 10000.0
RET_HEADS = 8
RET_DK = 128
RET_DV = 256
RET_QK_W = RET_HEADS * RET_DK
RET_V_W = RET_HEADS * RET_DV
RET_CHUNK = 128
PEER_HEADS = 8
PEER_NKEYS = 128
PEER_TOPK = 16
EPS = 1e-6
NEG = -1e30
F32_MIN = float(jnp.finfo(jnp.float32).min)

OFF_QA = 0
OFF_KA = ATT_W
OFF_VA = 2 * ATT_W
OFF_QR = 3 * ATT_W
OFF_KR = OFF_QR + RET_QK_W
OFF_VR = OFF_KR + RET_QK_W
OFF_GR = OFF_VR + RET_V_W
OFF_GA = OFF_GR + RET_V_W

BLOCK_UNROLL = 4
SLOT_BLOCK = 8
LANE_CHUNK = 256
VMEM_LIMIT = 56 * 1024 * 1024


def _cparams(sem):
    return pltpu.CompilerParams(dimension_semantics=sem, vmem_limit_bytes=VMEM_LIMIT)


def _sigmoid(x):
    return 1.0 / (1.0 + jnp.exp(-x))


def _rope_tables(pos, width):
    half = width // 2
    inv_freq = jnp.power(jnp.float32(ROPE_THETA), -jnp.arange(half, dtype=F32) / half)
    ang = pos.astype(F32)[:, None] * inv_freq[None, :]
    cos, sin = jnp.cos(ang), jnp.sin(ang)
    return jnp.concatenate([cos, cos], axis=-1), jnp.concatenate([-sin, sin], axis=-1)


def _rope(x, cos, sin):
    return x * cos + pltpu.roll(x, HD // 2, 1) * sin


def _head_rmsnorm(x, w):
    ms = jnp.mean(x * x, axis=-1, keepdims=True)
    return x * lax.rsqrt(ms + EPS) * w


def _norm_matmul_kernel(x_ref, nw_ref, w_ref, o_ref, xn_ref):
    @pl.when(pl.program_id(1) == 0)
    def _():
        x = x_ref[...]
        ms = jnp.mean(x * x, axis=-1, keepdims=True)
        xn_ref[...] = (x * lax.rsqrt(ms + EPS) * nw_ref[...]).astype(BF16)

    o_ref[...] = jnp.dot(xn_ref[...], w_ref[...], preferred_element_type=F32)


def _norm_matmul(x, norm_w, w_bf16, tm, tn):
    t, d = x.shape
    n = w_bf16.shape[1]
    tm = min(tm, t)
    return pl.pallas_call(
        _norm_matmul_kernel,
        grid=(t // tm, n // tn),
        in_specs=[
            pl.BlockSpec((tm, d), lambda i, j: (i, 0)),
            pl.BlockSpec((1, d), lambda i, j: (0, 0)),
            pl.BlockSpec((d, tn), lambda i, j: (0, j)),
        ],
        out_specs=pl.BlockSpec((tm, tn), lambda i, j: (i, j)),
        out_shape=jax.ShapeDtypeStruct((t, n), F32),
        scratch_shapes=[pltpu.VMEM((tm, d), BF16)],
        compiler_params=_cparams(("parallel", "arbitrary")),
        name="norm_inproj",
    )(x, norm_w.reshape(1, d), w_bf16)


def _attn_prompt_kernel(q0, k0, v0, q1, k1, v1, q2, k2, v2, cos_ref, sin_ref, qn_ref, kn_ref,
                        ko0, vo0, ko1, vo1, ko2, vo2, o_ref,
                        q_s, k_s, v_s, og_s, lse_s):
    L = q0.shape[0]
    blk = 128
    scale = HD ** -0.5
    cos = cos_ref[...]
    sin = sin_ref[...]
    k_s[pl.ds(0, L), :] = jnp.zeros((L, HD), F32)
    v_s[pl.ds(0, L), :] = jnp.zeros((L, HD), F32)
    groups = ((q0, k0, v0, ko0, vo0), (q1, k1, v1, ko1, vo1), (q2, k2, v2, ko2, vo2))
    for g, (qr, kr, vr, ko, vo) in enumerate(groups):
        win, dil = ATT_GROUPS[g]
        q = _rope(_head_rmsnorm(qr[...], qn_ref[...]), cos, sin) * scale
        k = _rope(_head_rmsnorm(kr[...], kn_ref[...]), cos, sin)
        v = vr[...]
        ko[...] = k
        vo[...] = v
        q_s[...] = q
        k_s[pl.ds(L, L), :] = k
        v_s[pl.ds(L, L), :] = v
        log_d = int(math.log2(dil))

        def body(t, carry, dil=dil, log_d=log_d, g=g):
            r = jnp.bitwise_and(t, dil - 1)
            i = lax.shift_right_logical(t, log_d)
            qstart = r + dil * blk * i
            kstart = L + qstart - dil * blk
            if dil == 1:
                qsl = pl.ds(qstart, blk)
                ksl = pl.ds(kstart, 2 * blk)
            else:
                qsl = pl.ds(qstart, blk, stride=dil)
                ksl = pl.ds(kstart, 2 * blk, stride=dil)
            qb = q_s[qsl, :].astype(BF16)
            kb = k_s[ksl, :].astype(BF16)
            vb = v_s[ksl, :].astype(BF16)
            s = lax.dot_general(qb, kb, (((1,), (1,)), ((), ())), preferred_element_type=F32)
            qi = lax.broadcasted_iota(jnp.int32, (blk, 2 * blk), 0)
            kj = lax.broadcasted_iota(jnp.int32, (blk, 2 * blk), 1)
            lo = jnp.maximum(qi, jnp.where(i == 0, blk, 0))
            valid = jnp.logical_and(kj >= lo, kj <= qi + blk)
            s = jnp.where(valid, s, NEG)
            m = jnp.max(s, axis=-1, keepdims=True)
            p = jnp.exp(s - m)
            den = jnp.sum(p, axis=-1, keepdims=True)
            o = jnp.dot((p / den).astype(BF16), vb, preferred_element_type=F32)
            lse = m + jnp.log(den)
            osl = pl.ds(g * L + qstart, blk) if dil == 1 else pl.ds(g * L + qstart, blk, stride=dil)
            og_s[osl, :] = o
            lse_s[osl, :] = jnp.broadcast_to(lse, (blk, HD))
            return carry

        lax.fori_loop(0, L // blk, body, 0, unroll=True)

    l0 = lse_s[pl.ds(0, L), :]
    l1 = lse_s[pl.ds(L, L), :]
    l2 = lse_s[pl.ds(2 * L, L), :]
    mx = jnp.maximum(jnp.maximum(l0, l1), l2)
    e0 = jnp.exp(l0 - mx)
    e1 = jnp.exp(l1 - mx)
    e2 = jnp.exp(l2 - mx)
    tot = e0 + e1 + e2
    o = ((e0 / tot) * og_s[pl.ds(0, L), :] + (e1 / tot) * og_s[pl.ds(L, L), :]
         + (e2 / tot) * og_s[pl.ds(2 * L, L), :])
    o_ref[...] = o.astype(o_ref.dtype)


def _attn_prompt(proj, cos, sin, qn, kn):
    b, l, _ = proj.shape
    in_specs = []
    for g in range(N_GROUPS):
        for off in (OFF_QA, OFF_KA, OFF_VA):
            cb = (off + g * ATT_OUT) // HD
            in_specs.append(pl.BlockSpec((None, l, HD), lambda bi, hi, cb=cb: (bi, 0, cb + hi)))
    in_specs += [
        pl.BlockSpec((l, HD), lambda bi, hi: (0, 0)),
        pl.BlockSpec((l, HD), lambda bi, hi: (0, 0)),
        pl.BlockSpec((1, HD), lambda bi, hi: (0, 0)),
        pl.BlockSpec((1, HD), lambda bi, hi: (0, 0)),
    ]
    head_spec = pl.BlockSpec((None, l, HD), lambda bi, hi: (bi, 0, hi))
    kv_shape = jax.ShapeDtypeStruct((b, l, ATT_OUT), F32)
    outs = pl.pallas_call(
        _attn_prompt_kernel,
        grid=(b, ATT_HEADS),
        in_specs=in_specs,
        out_specs=[head_spec] * 7,
        out_shape=[kv_shape] * 6 + [jax.ShapeDtypeStruct((b, l, ATT_OUT), BF16)],
        scratch_shapes=[
            pltpu.VMEM((l, HD), F32),
            pltpu.VMEM((2 * l, HD), F32),
            pltpu.VMEM((2 * l, HD), F32),
            pltpu.VMEM((N_GROUPS * l, HD), F32),
            pltpu.VMEM((N_GROUPS * l, HD), F32),
        ],
        compiler_params=_cparams(("parallel", "arbitrary")),
        name="attn_prompt",
    )(*([proj] * 9), cos, sin, qn, kn)
    return outs[:6], outs[6]


def _lane_sum(x):
    return jnp.broadcast_to(jnp.sum(x, axis=-1, keepdims=True), x.shape)


def _attn_sample_kernel(qa_ref, ka_ref, va_ref, c0_ref, c1_ref, c2_ref, cos_ref, sin_ref,
                        qn_ref, kn_ref, kv0_ref, kv1_ref, kv2_ref, o_ref, tile_s):
    nq = qa_ref.shape[0]
    scale = HD ** -0.5 * math.log2(math.e)
    cos = cos_ref[...]
    sin = sin_ref[...]
    qn = qn_ref[...]
    kn = kn_ref[...]
    caches = (c0_ref, c1_ref, c2_ref)
    kv_refs = (kv0_ref, kv1_ref, kv2_ref)

    def head_tiles(x):
        for h in range(ATT_HEADS):
            tile_s[pl.ds(h, nq, stride=ATT_HEADS), :] = x[:, h * HD:(h + 1) * HD]
        return [tile_s[i * ATT_HEADS:(i + 1) * ATT_HEADS, :] for i in range(nq)]

    o_groups = []
    lse_groups = []
    for g in range(N_GROUPS):
        win, dil = ATT_GROUPS[g]
        c_ref = caches[g]
        qs, ks = [], []
        for h in range(ATT_HEADS):
            col = g * ATT_OUT + h * HD
            qs.append(_rope(_head_rmsnorm(qa_ref[:, col:col + HD], qn), cos, sin) * scale)
            ks.append(_rope(_head_rmsnorm(ka_ref[:, col:col + HD], kn), cos, sin))
        q_g = jnp.concatenate(qs, axis=1)
        k_g = jnp.concatenate(ks, axis=1)
        v_g = va_ref[:, g * ATT_OUT:(g + 1) * ATT_OUT]
        kv_refs[g][:, 0:ATT_OUT] = k_g
        kv_refs[g][:, ATT_OUT:2 * ATT_OUT] = v_g
        qt = head_tiles(q_g)
        kt = head_tiles(k_g)
        vt = head_tiles(v_g)

        if g == 2:
            n_slots = c_ref.shape[0]
            load = lambda jj, kv, n: c_ref[jj, :, kv]
            n_valid = lambda jj: nq
        else:
            n_slots = win // dil
            load = lambda jj, kv, n, dil=dil: c_ref[pl.ds(dil * jj, n), kv]
            n_valid = lambda jj, dil=dil, win=win: min(nq, win - dil * jj)
        n_main = max(jj + 1 for jj in range(n_slots) if n_valid(jj) == nq)
        new_keys = [list(range(i, -1, -dil)) for i in range(nq)]

        def update(state, scores, values):
            m, den, acc = state
            m_new = jnp.maximum(m, functools.reduce(jnp.maximum, scores))
            a = jnp.exp2(m - m_new)
            ps = [jnp.exp2(s - m_new) for s in scores]
            den = den * a + functools.reduce(lambda x, y: x + y, ps)
            acc = acc * a + functools.reduce(lambda x, y: x + y, [p * v for p, v in zip(ps, values)])
            return m_new, den, acc

        states = []
        for i in range(nq):
            t0 = new_keys[i][0]
            st = (_lane_sum(qt[i] * kt[t0]), jnp.ones((ATT_HEADS, HD), F32), vt[t0])
            if len(new_keys[i]) > 1:
                ts = new_keys[i][1:]
                st = update(st, [_lane_sum(qt[i] * kt[t]) for t in ts], [vt[t] for t in ts])
            states.append(st)

        def block(slots, states):
            scores = [[] for _ in range(nq)]
            values = [[] for _ in range(nq)]
            for jj, valid in slots:
                kt8 = load(jj, 0, valid)
                vt8 = load(jj, 1, valid)
                for i in range(valid):
                    scores[i].append(_lane_sum(qt[i] * kt8[i]))
                    values[i].append(vt8[i])
            return tuple(update(states[i], scores[i], values[i]) if scores[i] else states[i]
                         for i in range(nq))

        n_blocks = n_main // SLOT_BLOCK
        states = lax.fori_loop(
            0, n_blocks,
            lambda bi, st: block([(bi * SLOT_BLOCK + j, nq) for j in range(SLOT_BLOCK)], st),
            tuple(states), unroll=BLOCK_UNROLL)
        rest = [(jj, n_valid(jj)) for jj in range(n_blocks * SLOT_BLOCK, n_slots)]
        for k in range(0, len(rest), SLOT_BLOCK):
            states = block(rest[k:k + SLOT_BLOCK], states)
        o_groups.append([acc / den for (_, den, acc) in states])
        lse_groups.append([m * math.log(2.0) + jnp.log(den) for (m, den, _) in states])

    for i in range(nq):
        l0, l1, l2 = lse_groups[0][i], lse_groups[1][i], lse_groups[2][i]
        mx = jnp.maximum(jnp.maximum(l0, l1), l2)
        e0, e1, e2 = jnp.exp(l0 - mx), jnp.exp(l1 - mx), jnp.exp(l2 - mx)
        tot = e0 + e1 + e2
        o_ref[i] = ((e0 / tot) * o_groups[0][i] + (e1 / tot) * o_groups[1][i]
                    + (e2 / tot) * o_groups[2][i])


def _attn_sample(proj, c0, c1, c2, cos, sin, qn, kn):
    b, nq, _ = proj.shape
    kvw = 2 * ATT_OUT
    small = lambda shape: pl.BlockSpec(shape, lambda bi: (0, 0))
    tail = (2, ATT_HEADS, HD)
    outs = pl.pallas_call(
        _attn_sample_kernel,
        grid=(b,),
        in_specs=[
            pl.BlockSpec((None, nq, ATT_W), lambda bi: (bi, 0, OFF_QA // ATT_W)),
            pl.BlockSpec((None, nq, ATT_W), lambda bi: (bi, 0, OFF_KA // ATT_W)),
            pl.BlockSpec((None, nq, ATT_W), lambda bi: (bi, 0, OFF_VA // ATT_W)),
            pl.BlockSpec((None, None, c0.shape[2]) + tail, lambda bi: (0, bi, 0, 0, 0, 0)),
            pl.BlockSpec((None, None, c1.shape[2]) + tail, lambda bi: (0, bi, 0, 0, 0, 0)),
            pl.BlockSpec((None, None, c2.shape[2], nq) + tail, lambda bi: (0, bi, 0, 0, 0, 0, 0)),
            small((nq, HD)), small((nq, HD)), small((1, HD)), small((1, HD)),
        ],
        out_specs=[pl.BlockSpec((None, nq, kvw), lambda bi: (bi, 0, 0))] * 3
        + [pl.BlockSpec((None, nq, ATT_HEADS, HD), lambda bi: (bi, 0, 0, 0))],
        out_shape=[jax.ShapeDtypeStruct((b, nq, kvw), F32)] * 3
        + [jax.ShapeDtypeStruct((b, nq, ATT_HEADS, HD), F32)],
        scratch_shapes=[pltpu.VMEM((nq * ATT_HEADS, HD), F32)],
        compiler_params=_cparams(("parallel",)),
        name="attn_sample",
    )(proj, proj, proj, c0, c1, c2, cos, sin, qn, kn)
    return outs[:3], outs[3]


def _ret_tables(c):
    log_g = jnp.log1p(-jnp.exp2(-5.0 - jnp.arange(RET_HEADS, dtype=F32)))
    ii = jnp.arange(c, dtype=F32)
    diff = ii[:, None] - ii[None, :]
    inner = jnp.where(diff[None] >= 0, jnp.exp(jnp.maximum(diff, 0.0)[None] * log_g[:, None, None]), 0.0)
    q_dec = jnp.exp((ii + 1.0)[None, :] * log_g[:, None])
    k_dec = jnp.exp((c - 1.0 - ii)[None, :] * log_g[:, None])
    chunk_dec = jnp.exp(c * log_g)
    return inner, q_dec, k_dec, chunk_dec


def _ret_prompt_kernel(q_ref, k_ref, v_ref, g_ref, cos_ref, sin_ref, inner_ref, qd_ref, kd_ref, cd_ref,
                       o_ref, st_ref, q_s, k_s, s_s):
    L = q_ref.shape[0]
    C = RET_CHUNK
    q_s[...] = _rope(q_ref[...], cos_ref[...], sin_ref[...])
    k_s[...] = _rope(k_ref[...], cos_ref[...], sin_ref[...]) * (RET_DK ** -0.5)
    s_s[...] = jnp.zeros(s_s.shape, F32)
    inner = inner_ref[...]
    qd = qd_ref[...]
    kd = kd_ref[...]
    cd = cd_ref[...]

    def body(ci, carry):
        sl = pl.ds(pl.multiple_of(ci * C, C), C)
        qc = q_s[sl, :]
        kc = k_s[sl, :]
        vc = v_ref[sl, :].astype(BF16)
        state = s_s[...]
        sc = lax.dot_general(qc.astype(BF16), kc.astype(BF16), (((1,), (1,)), ((), ())),
                             preferred_element_type=F32) * inner
        o = (jnp.dot(sc.astype(BF16), vc, preferred_element_type=F32)
             + jnp.dot((qc * qd).astype(BF16), state.astype(BF16), preferred_element_type=F32))
        kdt = jnp.transpose(kc * kd).astype(BF16)
        s_s[...] = state * cd + jnp.dot(kdt, vc, preferred_element_type=F32)
        ms = jnp.mean(o * o, axis=-1, keepdims=True)
        gate = g_ref[sl, :]
        o_ref[sl, :] = (o * lax.rsqrt(ms + EPS) * (gate * _sigmoid(gate))).astype(o_ref.dtype)
        return carry

    lax.fori_loop(0, L // C, body, 0)
    st_ref[...] = s_s[...]


def _ret_prompt(proj, cos, sin):
    b, l, _ = proj.shape
    inner, q_dec, k_dec, chunk_dec = _ret_tables(RET_CHUNK)
    qd = jnp.broadcast_to(q_dec[:, :, None], (RET_HEADS, RET_CHUNK, RET_DK))
    kd = jnp.broadcast_to(k_dec[:, :, None], (RET_HEADS, RET_CHUNK, RET_DK))
    cd = jnp.broadcast_to(chunk_dec[:, None, None], (RET_HEADS, RET_DK, RET_DV))
    per_head = lambda r, c: pl.BlockSpec((None, r, c), lambda bi, hi: (hi, 0, 0))
    return pl.pallas_call(
        _ret_prompt_kernel,
        grid=(b, RET_HEADS),
        in_specs=[
            pl.BlockSpec((None, l, RET_DK), lambda bi, hi: (bi, 0, OFF_QR // RET_DK + hi)),
            pl.BlockSpec((None, l, RET_DK), lambda bi, hi: (bi, 0, OFF_KR // RET_DK + hi)),
            pl.BlockSpec((None, l, RET_DV), lambda bi, hi: (bi, 0, OFF_VR // RET_DV + hi)),
            pl.BlockSpec((None, l, RET_DV), lambda bi, hi: (bi, 0, OFF_GR // RET_DV + hi)),
            pl.BlockSpec((l, RET_DK), lambda bi, hi: (0, 0)),
            pl.BlockSpec((l, RET_DK), lambda bi, hi: (0, 0)),
            per_head(RET_CHUNK, RET_CHUNK), per_head(RET_CHUNK, RET_DK), per_head(RET_CHUNK, RET_DK),
            per_head(RET_DK, RET_DV),
        ],
        out_specs=[
            pl.BlockSpec((None, l, RET_DV), lambda bi, hi: (bi, 0, hi)),
            pl.BlockSpec((None, None, RET_DK, RET_DV), lambda bi, hi: (bi, hi, 0, 0)),
        ],
        out_shape=[jax.ShapeDtypeStruct((b, l, RET_V_W), BF16),
                   jax.ShapeDtypeStruct((b, RET_HEADS, RET_DK, RET_DV), F32)],
        scratch_shapes=[pltpu.VMEM((l, RET_DK), F32), pltpu.VMEM((l, RET_DK), F32),
                        pltpu.VMEM((RET_DK, RET_DV), F32)],
        compiler_params=_cparams(("parallel", "arbitrary")),
        name="ret_prompt",
    )(proj, proj, proj, proj, cos, sin, inner, qd, kd, cd)


def _ret_sample_kernel(a_ref, b_ref, s0_ref, cos_ref, sin_ref, inner_ref, qd_ref, kd_ref, cd_ref,
                       o_ref, st_ref):
    nq = a_ref.shape[0]
    cos = cos_ref[...]
    sin = sin_ref[...]
    pad = 128 - nq
    for h in range(RET_HEADS):
        q = _rope(a_ref[:, h * RET_DK:(h + 1) * RET_DK], cos, sin)
        k = _rope(a_ref[:, RET_QK_W + h * RET_DK:RET_QK_W + (h + 1) * RET_DK], cos, sin) * (RET_DK ** -0.5)
        vc0 = h * RET_DV
        if vc0 < ATT_OUT:
            v = a_ref[:, 2 * RET_QK_W + vc0:2 * RET_QK_W + vc0 + RET_DV]
        else:
            v = b_ref[:, vc0 - ATT_OUT:vc0 - ATT_OUT + RET_DV]
        gate = b_ref[:, ATT_OUT + h * RET_DV:ATT_OUT + (h + 1) * RET_DV]
        state = s0_ref[h]
        kp = jnp.concatenate([k, jnp.zeros((pad, RET_DK), F32)], axis=0)
        vp = jnp.concatenate([v, jnp.zeros((pad, RET_DV), F32)], axis=0).astype(BF16)
        sc = lax.dot_general(q.astype(BF16), kp.astype(BF16), (((1,), (1,)), ((), ())),
                             preferred_element_type=F32) * inner_ref[h]
        o = (jnp.dot(sc.astype(BF16), vp, preferred_element_type=F32)
             + jnp.dot((q * qd_ref[h]).astype(BF16), state.astype(BF16), preferred_element_type=F32))
        kdt = jnp.transpose(kp * kd_ref[h]).astype(BF16)
        st_ref[h] = state * cd_ref[h] + jnp.dot(kdt, vp, preferred_element_type=F32)
        ms = jnp.mean(o * o, axis=-1, keepdims=True)
        o_ref[:, h * RET_DV:(h + 1) * RET_DV] = o * lax.rsqrt(ms + EPS) * (gate * _sigmoid(gate))


def _ret_sample(proj, state, cos, sin):
    b, nq, _ = proj.shape
    inner, q_dec, k_dec, chunk_dec = _ret_tables(math.gcd(nq, RET_CHUNK))
    pad = 128 - nq
    inner_p = jnp.pad(inner, ((0, 0), (0, 0), (0, pad)))
    qd = jnp.broadcast_to(q_dec[:, :, None], (RET_HEADS, nq, RET_DK))
    kd = jnp.broadcast_to(jnp.pad(k_dec, ((0, 0), (0, pad)))[:, :, None], (RET_HEADS, 128, RET_DK))
    cd = jnp.broadcast_to(chunk_dec[:, None, None], (RET_HEADS, RET_DK, RET_DV))
    full = lambda a: pl.BlockSpec(a.shape, lambda bi: (0,) * a.ndim)
    return pl.pallas_call(
        _ret_sample_kernel,
        grid=(b,),
        in_specs=[
            pl.BlockSpec((None, nq, ATT_W), lambda bi: (bi, 0, OFF_QR // ATT_W)),
            pl.BlockSpec((None, nq, ATT_W), lambda bi: (bi, 0, OFF_QR // ATT_W + 1)),
            pl.BlockSpec((None, RET_HEADS, RET_DK, RET_DV), lambda bi: (bi, 0, 0, 0)),
            full(cos), full(sin), full(inner_p), full(qd), full(kd), full(cd),
        ],
        out_specs=[
            pl.BlockSpec((None, nq, RET_V_W), lambda bi: (bi, 0, 0)),
            pl.BlockSpec((None, RET_HEADS, RET_DK, RET_DV), lambda bi: (bi, 0, 0, 0)),
        ],
        out_shape=[jax.ShapeDtypeStruct((b, nq, RET_V_W), F32),
                   jax.ShapeDtypeStruct((b, RET_HEADS, RET_DK, RET_DV), F32)],
        compiler_params=_cparams(("parallel",)),
        name="ret_sample",
    )(proj, proj, state, cos, sin, inner_p, qd, kd, cd)


def _branch_merge_kernel(oa_ref, or_ref, ga_ref, gb_ref, wa_ref, wr_ref, o_ref):
    a = jnp.dot(oa_ref[...], wa_ref[...], preferred_element_type=F32)
    r = jnp.dot(or_ref[...], wr_ref[...], preferred_element_type=F32)
    o_ref[...] = (_sigmoid(ga_ref[...]) * a + _sigmoid(gb_ref[...]) * r).astype(o_ref.dtype)


def _branch_merge(o_att, o_ret, proj, w_att, w_ret, tm, tn):
    t = o_att.shape[0]
    n = w_att.shape[1]
    tm = min(tm, t)
    return pl.pallas_call(
        _branch_merge_kernel,
        grid=(t // tm, n // tn),
        in_specs=[
            pl.BlockSpec((tm, o_att.shape[1]), lambda i, j: (i, 0)),
            pl.BlockSpec((tm, o_ret.shape[1]), lambda i, j: (i, 0)),
            pl.BlockSpec((tm, tn), lambda i, j: (i, OFF_GA // tn + j)),
            pl.BlockSpec((tm, tn), lambda i, j: (i, (OFF_GA + n) // tn + j)),
            pl.BlockSpec((w_att.shape[0], tn), lambda i, j: (0, j)),
            pl.BlockSpec((w_ret.shape[0], tn), lambda i, j: (0, j)),
        ],
        out_specs=pl.BlockSpec((tm, tn), lambda i, j: (i, j)),
        out_shape=jax.ShapeDtypeStruct((t, n), BF16),
        compiler_params=_cparams(("parallel", "arbitrary")),
        name="branch_merge",
    )(o_att, o_ret, proj, proj, w_att, w_ret)


def _residual_matmul_kernel(x_ref, m_ref, w_ref, o_ref):
    o_ref[...] = x_ref[...] + jnp.dot(m_ref[...], w_ref[...], preferred_element_type=F32)


def _residual_matmul(x, mixed, w, tm, tn):
    t, d = x.shape
    tm = min(tm, t)
    return pl.pallas_call(
        _residual_matmul_kernel,
        grid=(t // tm, d // tn),
        in_specs=[
            pl.BlockSpec((tm, tn), lambda i, j: (i, j)),
            pl.BlockSpec((tm, mixed.shape[1]), lambda i, j: (i, 0)),
            pl.BlockSpec((w.shape[0], tn), lambda i, j: (0, j)),
        ],
        out_specs=pl.BlockSpec((tm, tn), lambda i, j: (i, j)),
        out_shape=jax.ShapeDtypeStruct((t, d), F32),
        compiler_params=_cparams(("parallel", "arbitrary")),
        name="mix_out_residual",
    )(x, mixed, w)


def _extract_top(x, n):
    tops = []
    for _ in range(n):
        m = jnp.max(x, axis=0, keepdims=True)
        tops.append(m)
        x = jnp.where(x == m, F32_MIN, x)
    return tops


def _peer_route_kernel(h_ref, nw_ref, wq_ref, sk_ref, hn_ref, s1_ref, s2_ref, e1_ref, e2_ref, tau_ref):
    x = h_ref[...]
    ms = jnp.mean(x * x, axis=-1, keepdims=True)
    hn = (x * lax.rsqrt(ms + EPS) * nw_ref[...]).astype(BF16)
    hn_ref[...] = hn
    q = jnp.dot(hn, wq_ref[...], preferred_element_type=F32).astype(BF16)
    half = PEER_NKEYS
    for h in range(PEER_HEADS):
        tops = []
        for p in range(2):
            hp = 2 * h + p
            s_t = lax.dot_general(sk_ref[hp], q[:, hp * half:(hp + 1) * half], (((1,), (1,)), ((), ())),
                                  preferred_element_type=F32)
            top = _extract_top(s_t, PEER_TOPK + 1)
            tops.append(top)
            e = jnp.exp(s_t - top[0])
            if p == 0:
                s1_ref[h] = s_t
                e1_ref[h] = e
            else:
                s2_ref[h] = s_t
                e2_raw = e
        t2 = jnp.concatenate(tops[1][:PEER_TOPK], axis=0)
        cand = jnp.concatenate([tops[0][a] + t2 for a in range(PEER_TOPK)], axis=0)
        best = _extract_top(cand, PEER_TOPK + 1)
        z = jnp.zeros_like(best[0])
        for c in best[:PEER_TOPK]:
            z = z + jnp.exp(c - best[0])
        e2_ref[h] = e2_raw / z
        runner_up = jnp.maximum(best[PEER_TOPK], jnp.maximum(tops[0][PEER_TOPK] + tops[1][0],
                                                             tops[0][0] + tops[1][PEER_TOPK]))
        tau_ref[pl.ds(h, 1), :] = 0.5 * (best[PEER_TOPK - 1] + runner_up)


def _peer_route(h, norm_w, wq, sub_keys, tm):
    t, d = h.shape
    tm = min(tm, t)
    stat = jax.ShapeDtypeStruct((PEER_HEADS, PEER_NKEYS, t), F32)
    stat_spec = pl.BlockSpec((PEER_HEADS, PEER_NKEYS, tm), lambda i: (0, 0, i))
    return pl.pallas_call(
        _peer_route_kernel,
        grid=(t // tm,),
        in_specs=[
            pl.BlockSpec((tm, d), lambda i: (i, 0)),
            pl.BlockSpec((1, d), lambda i: (0, 0)),
            pl.BlockSpec(wq.shape, lambda i: (0, 0)),
            pl.BlockSpec(sub_keys.shape, lambda i: (0, 0, 0)),
        ],
        out_specs=[pl.BlockSpec((tm, d), lambda i: (i, 0))] + [stat_spec] * 4
        + [pl.BlockSpec((PEER_HEADS, tm), lambda i: (0, i))],
        out_shape=[jax.ShapeDtypeStruct((t, d), BF16)] + [stat] * 4
        + [jax.ShapeDtypeStruct((PEER_HEADS, t), F32)],
        compiler_params=_cparams(("parallel",)),
        name="peer_route",
    )(h, norm_w.reshape(1, d), wq, sub_keys)


def _gelu_tanh(x):
    c = math.sqrt(2.0 / math.pi)
    inner = x * (c + (c * 0.044715) * (x * x))
    half = 0.5 * x
    return half + half * jnp.tanh(inner)


def _peer_expert_kernel(hn_ref, u_ref, v_ref, s1_ref, e1_ref, s2_ref, e2_ref, tau_ref, h_ref, y_ref,
                        act_s, g_s):
    e_tile = u_ref.shape[0]
    tm = hn_ref.shape[0]
    nk = PEER_NKEYS
    lc = min(LANE_CHUNK, tm)

    @pl.when(pl.program_id(1) == 0)
    def _():
        y_ref[...] = h_ref[...]

    act_s[...] = lax.dot_general(u_ref[...], hn_ref[...], (((1,), (1,)), ((), ())),
                                 preferred_element_type=F32)

    def body(a, carry):
        rows = pl.ds(pl.multiple_of(a * nk, nk), nk)
        thr = [tau_ref[h:h + 1, :] - s1_ref[h, pl.ds(a, 1), :] for h in range(PEER_HEADS)]
        e1rows = [e1_ref[h, pl.ds(a, 1), :] for h in range(PEER_HEADS)]
        for tcol in range(tm // lc):
            lanes = slice(tcol * lc, (tcol + 1) * lc)
            w = jnp.zeros((nk, lc), F32)
            for h in range(PEER_HEADS):
                gate = e2_ref[h, :, lanes] * e1rows[h][:, lanes]
                w = w + jnp.where(s2_ref[h, :, lanes] >= thr[h][:, lanes], gate, 0.0)
            g_s[rows, lanes] = (w * _gelu_tanh(act_s[rows, lanes])).astype(BF16)
        return carry

    lax.fori_loop(0, e_tile // nk, body, 0)
    y_ref[...] += lax.dot_general(g_s[...], v_ref[...], (((0,), (0,)), ((), ())),
                                  preferred_element_type=F32)


def _peer_expert(hn, u, v, s1, s2, e1, e2, tau, h, tm, e_tile):
    t, d = h.shape
    tm = min(tm, t)
    n_exp = u.shape[0]
    a_tile = e_tile // PEER_NKEYS
    row_spec = pl.BlockSpec((tm, d), lambda i, e: (i, 0))
    a_spec = pl.BlockSpec((PEER_HEADS, a_tile, tm), lambda i, e: (0, e, i))
    b_spec = pl.BlockSpec((PEER_HEADS, PEER_NKEYS, tm), lambda i, e: (0, 0, i))
    return pl.pallas_call(
        _peer_expert_kernel,
        grid=(t // tm, n_exp // e_tile),
        in_specs=[
            row_spec,
            pl.BlockSpec((e_tile, d), lambda i, e: (e, 0)),
            pl.BlockSpec((e_tile, d), lambda i, e: (e, 0)),
            a_spec, a_spec, b_spec, b_spec,
            pl.BlockSpec((PEER_HEADS, tm), lambda i, e: (0, i)),
            row_spec,
        ],
        out_specs=row_spec,
        out_shape=jax.ShapeDtypeStruct((t, d), F32),
        scratch_shapes=[pltpu.VMEM((e_tile, tm), F32), pltpu.VMEM((e_tile, tm), BF16)],
        compiler_params=_cparams(("parallel", "arbitrary")),
        name="peer_expert",
    )(hn, u, v, s1, e1, s2, e2, tau, h)


def _token_tail(x, proj, o_att, o_ret, w, tm):
    mixed = _branch_merge(o_att, o_ret, proj, w["att_br"], w["ret_br"], tm, 512)
    h = _residual_matmul(x, mixed, w["mix_out"], tm, 512)
    hn, s1, s2, e1, e2, tau = _peer_route(h, w["norm2"], w["peer_q"], w["sub_keys"], 256)
    return _peer_expert(hn, w["peer_u"], w["peer_v"], s1, s2, e1, e2, tau, h, 512, 1024)


def kernel(x_prompt, x_sample, cache_att_w128, cache_att_w512, cache_att_w2048, state_ret, norm1_w, w_in,
           att_q_norm_w, att_k_norm_w, w_att_br, w_ret_br, w_mix_out, norm2_w, peer_w_q, peer_sub_keys,
           peer_u, peer_v):
    depth = w_in.shape[0]
    assert depth == 1, "single-layer trunk"
    b, l, d = x_prompt.shape
    sb, sl, _ = x_sample.shape
    past = cache_att_w2048.shape[2]
    w = {
        "att_br": w_att_br[0].astype(BF16),
        "ret_br": w_ret_br[0].astype(BF16),
        "mix_out": w_mix_out[0].astype(BF16),
        "norm2": norm2_w[0],
        "peer_q": peer_w_q[0].astype(BF16),
        "sub_keys": peer_sub_keys[0].reshape(PEER_HEADS * 2, PEER_NKEYS, -1).astype(BF16),
        "peer_u": peer_u[0].astype(BF16),
        "peer_v": peer_v[0].astype(BF16),
    }
    w_in_b = w_in[0].astype(BF16)
    qn = att_q_norm_w[0].reshape(1, HD)
    kn = att_k_norm_w[0].reshape(1, HD)
    xp = x_prompt.reshape(b * l, d)
    xs = x_sample.reshape(sb * sl, d)

    proj_p = _norm_matmul(xp, norm1_w[0], w_in_b, 1024, 512)
    proj_s = _norm_matmul(xs, norm1_w[0], w_in_b, 1024, 512)
    in_w = proj_p.shape[1]

    cos_p, sin_p = _rope_tables(jnp.arange(l, dtype=jnp.int32), HD)
    kv_p, o_att_p = _attn_prompt(proj_p.reshape(b, l, in_w), cos_p, sin_p, qn, kn)
    o_ret_p, st_p = _ret_prompt(proj_p.reshape(b, l, in_w), cos_p, sin_p)
    y_p = _token_tail(xp, proj_p, o_att_p.reshape(b * l, ATT_OUT), o_ret_p.reshape(b * l, RET_V_W), w, 1024)

    cos_s, sin_s = _rope_tables(past + jnp.arange(sl, dtype=jnp.int32), HD)
    kvw = 2 * ATT_OUT
    c2 = cache_att_w2048.reshape(1, sb, past // 16, 16, 2, ATT_HEADS, HD)
    kv_s, o_att_s = _attn_sample(proj_s.reshape(sb, sl, in_w), cache_att_w128, cache_att_w512, c2,
                                 cos_s, sin_s, qn, kn)
    o_ret_s, st_s = _ret_sample(proj_s.reshape(sb, sl, in_w), state_ret[0], cos_s, sin_s)
    y_s = _token_tail(xs, proj_s, o_att_s.reshape(sb * sl, ATT_OUT).astype(BF16),
                      o_ret_s.reshape(sb * sl, RET_V_W).astype(BF16), w, 1024)

    new_p = []
    for g, (win, _) in enumerate(ATT_GROUPS):
        rows = min(win, l)
        kk = kv_p[2 * g][:, l - rows:].reshape(b, rows, 1, ATT_HEADS, HD)
        vv = kv_p[2 * g + 1][:, l - rows:].reshape(b, rows, 1, ATT_HEADS, HD)
        new_p.append(jnp.concatenate([kk, vv], axis=2)[None])
    new_s = [kv.reshape(1, sb, sl, 2, ATT_HEADS, HD) for kv in kv_s]
    return (y_p.reshape(b, l, d), y_s.reshape(sb, sl, d),
            new_p[0], new_p[1], new_p[2], st_p[None],
            new_s[0], new_s[1], new_s[2], st_s[None])
```

```python
import functools
import math

import jax
import jax.numpy as jnp
from jax import lax
from jax.experimental import pallas as pl
from jax.experimental.pallas import tpu as pltpu

F32 = jnp.float32
BF16 = jnp.bfloat16

ATT_GROUPS = ((128, 1), (512, 4), (2048, 16))
N_GROUPS = 3
ATT_HEADS = 8
HD = 128
ATT_W = N_GROUPS * ATT_HEADS * HD
ATT_OUT = ATT_HEADS * HD
N_KEYS = 129
ROPE_THETA = 10000.0
RET_HEADS = 8
RET_DK = 128
RET_DV = 256
RET_QK_W = RET_HEADS * RET_DK
RET_V_W = RET_HEADS * RET_DV
RET_CHUNK = 128
PEER_HEADS = 8
PEER_NKEYS = 128
PEER_TOPK = 16
EPS = 1e-6
NEG = -1e30
F32_MIN = float(jnp.finfo(jnp.float32).min)

OFF_QA = 0
OFF_KA = ATT_W
OFF_VA = 2 * ATT_W
OFF_QR = 3 * ATT_W
OFF_KR = OFF_QR + RET_QK_W
OFF_VR = OFF_KR + RET_QK_W
OFF_GR = OFF_VR + RET_V_W
OFF_GA = OFF_GR + RET_V_W

BLOCK_UNROLL = 4
SLOT_BLOCK = 8
LANE_CHUNK = 256
VMEM_LIMIT = 56 * 1024 * 1024


def _cparams(sem):
    return pltpu.CompilerParams(dimension_semantics=sem, vmem_limit_bytes=VMEM_LIMIT)


def _sigmoid(x):
    return 1.0 / (1.0 + jnp.exp(-x))


def _rope_tables(pos, width):
    half = width // 2
    inv_freq = jnp.power(jnp.float32(ROPE_THETA), -jnp.arange(half, dtype=F32) / half)
    ang = pos.astype(F32)[:, None] * inv_freq[None, :]
    cos, sin = jnp.cos(ang), jnp.sin(ang)
    return jnp.concatenate([cos, cos], axis=-1), jnp.concatenate([-sin, sin], axis=-1)


def _rope(x, cos, sin):
    return x * cos + pltpu.roll(x, HD // 2, 1) * sin


def _head_rmsnorm(x, w):
    ms = jnp.mean(x * x, axis=-1, keepdims=True)
    return x * lax.rsqrt(ms + EPS) * w


def _norm_matmul_kernel(x_ref, nw_ref, w_ref, o_ref, xn_ref):
    @pl.when(pl.program_id(1) == 0)
    def _():
        x = x_ref[...]
        ms = jnp.mean(x * x, axis=-1, keepdims=True)
        xn_ref[...] = (x * lax.rsqrt(ms + EPS) * nw_ref[...]).astype(BF16)

    o_ref[...] = jnp.dot(xn_ref[...], w_ref[...], preferred_element_type=F32)


def _norm_matmul(x, norm_w, w_bf16, tm, tn):
    t, d = x.shape
    n = w_bf16.shape[1]
    tm = min(tm, t)
    return pl.pallas_call(
        _norm_matmul_kernel,
        grid=(t // tm, n // tn),
        in_specs=[
            pl.BlockSpec((tm, d), lambda i, j: (i, 0)),
            pl.BlockSpec((1, d), lambda i, j: (0, 0)),
            pl.BlockSpec((d, tn), lambda i, j: (0, j)),
        ],
        out_specs=pl.BlockSpec((tm, tn), lambda i, j: (i, j)),
        out_shape=jax.ShapeDtypeStruct((t, n), F32),
        scratch_shapes=[pltpu.VMEM((tm, d), BF16)],
        compiler_params=_cparams(("parallel", "arbitrary")),
        name="norm_inproj",
    )(x, norm_w.reshape(1, d), w_bf16)


def _attn_prompt_kernel(q0, k0, v0, q1, k1, v1, q2, k2, v2, cos_ref, sin_ref, qn_ref, kn_ref,
                        ko0, vo0, ko1, vo1, ko2, vo2, o_ref,
                        q_s, k_s, v_s, og_s, lse_s):
    L = q0.shape[0]
    blk = 128
    scale = HD ** -0.5
    cos = cos_ref[...]
    sin = sin_ref[...]
    k_s[pl.ds(0, L), :] = jnp.zeros((L, HD), F32)
    v_s[pl.ds(0, L), :] = jnp.zeros((L, HD), F32)
    groups = ((q0, k0, v0, ko0, vo0), (q1, k1, v1, ko1, vo1), (q2, k2, v2, ko2, vo2))
    for g, (qr, kr, vr, ko, vo) in enumerate(groups):
        win, dil = ATT_GROUPS[g]
        q = _rope(_head_rmsnorm(qr[...], qn_ref[...]), cos, sin) * scale
        k = _rope(_head_rmsnorm(kr[...], kn_ref[...]), cos, sin)
        v = vr[...]
        ko[...] = k
        vo[...] = v
        q_s[...] = q
        k_s[pl.ds(L, L), :] = k
        v_s[pl.ds(L, L), :] = v
        log_d = int(math.log2(dil))

        def body(t, carry, dil=dil, log_d=log_d, g=g):
            r = jnp.bitwise_and(t, dil - 1)
            i = lax.shift_right_logical(t, log_d)
            qstart = r + dil * blk * i
            kstart = L + qstart - dil * blk
            if dil == 1:
                qsl = pl.ds(qstart, blk)
                ksl = pl.ds(kstart, 2 * blk)
            else:
                qsl = pl.ds(qstart, blk, stride=dil)
                ksl = pl.ds(kstart, 2 * blk, stride=dil)
            qb = q_s[qsl, :].astype(BF16)
            kb = k_s[ksl, :].astype(BF16)
            vb = v_s[ksl, :].astype(BF16)
            s = lax.dot_general(qb, kb, (((1,), (1,)), ((), ())), preferred_element_type=F32)
            qi = lax.broadcasted_iota(jnp.int32, (blk, 2 * blk), 0)
            kj = lax.broadcasted_iota(jnp.int32, (blk, 2 * blk), 1)
            lo = jnp.maximum(qi, jnp.where(i == 0, blk, 0))
            valid = jnp.logical_and(kj >= lo, kj <= qi + blk)
            s = jnp.where(valid, s, NEG)
            m = jnp.max(s, axis=-1, keepdims=True)
            p = jnp.exp(s - m)
            den = jnp.sum(p, axis=-1, keepdims=True)
            o = jnp.dot((p / den).astype(BF16), vb, preferred_element_type=F32)
            lse = m + jnp.log(den)
            osl = pl.ds(g * L + qstart, blk) if dil == 1 else pl.ds(g * L + qstart, blk, stride=dil)
            og_s[osl, :] = o
            lse_s[osl, :] = jnp.broadcast_to(lse, (blk, HD))
            return carry

        lax.fori_loop(0, L // blk, body, 0, unroll=True)

    l0 = lse_s[pl.ds(0, L), :]
    l1 = lse_s[pl.ds(L, L), :]
    l2 = lse_s[pl.ds(2 * L, L), :]
    mx = jnp.maximum(jnp.maximum(l0, l1), l2)
    e0 = jnp.exp(l0 - mx)
    e1 = jnp.exp(l1 - mx)
    e2 = jnp.exp(l2 - mx)
    tot = e0 + e1 + e2
    o = ((e0 / tot) * og_s[pl.ds(0, L), :] + (e1 / tot) * og_s[pl.ds(L, L), :]
         + (e2 / tot) * og_s[pl.ds(2 * L, L), :])
    o_ref[...] = o.astype(o_ref.dtype)


def _attn_prompt(proj, cos, sin, qn, kn):
    b, l, _ = proj.shape
    in_specs = []
    for g in range(N_GROUPS):
        for off in (OFF_QA, OFF_KA, OFF_VA):
            cb = (off + g * ATT_OUT) // HD
            in_specs.append(pl.BlockSpec((None, l, HD), lambda bi, hi, cb=cb: (bi, 0, cb + hi)))
    in_specs += [
        pl.BlockSpec((l, HD), lambda bi, hi: (0, 0)),
        pl.BlockSpec((l, HD), lambda bi, hi: (0, 0)),
        pl.BlockSpec((1, HD), lambda bi, hi: (0, 0)),
        pl.BlockSpec((1, HD), lambda bi, hi: (0, 0)),
    ]
    head_spec = pl.BlockSpec((None, l, HD), lambda bi, hi: (bi, 0, hi))
    kv_shape = jax.ShapeDtypeStruct((b, l, ATT_OUT), F32)
    outs = pl.pallas_call(
        _attn_prompt_kernel,
        grid=(b, ATT_HEADS),
        in_specs=in_specs,
        out_specs=[head_spec] * 7,
        out_shape=[kv_shape] * 6 + [jax.ShapeDtypeStruct((b, l, ATT_OUT), BF16)],
        scratch_shapes=[
            pltpu.VMEM((l, HD), F32),
            pltpu.VMEM((2 * l, HD), F32),
            pltpu.VMEM((2 * l, HD), F32),
            pltpu.VMEM((N_GROUPS * l, HD), F32),
            pltpu.VMEM((N_GROUPS * l, HD), F32),
        ],
        compiler_params=_cparams(("parallel", "arbitrary")),
        name="attn_prompt",
    )(*([proj] * 9), cos, sin, qn, kn)
    return outs[:6], outs[6]


def _lane_sum(x):
    return jnp.broadcast_to(jnp.sum(x, axis=-1, keepdims=True), x.shape)


def _attn_sample_kernel(qa_ref, ka_ref, va_ref, c0_ref, c1_ref, c2_ref, cos_ref, sin_ref,
                        qn_ref, kn_ref, kv0_ref, kv1_ref, kv2_ref, o_ref, tile_q, tile_k, tile_v):
    nq = qa_ref.shape[0]
    scale = HD ** -0.5 * math.log2(math.e)
    cos = cos_ref[...]
    sin = sin_ref[...]
    qn = qn_ref[...]
    kn = kn_ref[...]
    caches = (c0_ref, c1_ref, c2_ref)
    kv_refs = (kv0_ref, kv1_ref, kv2_ref)

    def head_tiles(x, tile_s):
        for h in range(ATT_HEADS):
            tile_s[pl.ds(h, nq, stride=ATT_HEADS), :] = x[:, h * HD:(h + 1) * HD]
        return [tile_s[i * ATT_HEADS:(i + 1) * ATT_HEADS, :] for i in range(nq)]

    o_groups = []
    lse_groups = []
    for g in range(N_GROUPS):
        win, dil = ATT_GROUPS[g]
        c_ref = caches[g]
        qs, ks = [], []
        for h in range(ATT_HEADS):
            col = g * ATT_OUT + h * HD
            qs.append(_rope(_head_rmsnorm(qa_ref[:, col:col + HD], qn), cos, sin) * scale)
            ks.append(_rope(_head_rmsnorm(ka_ref[:, col:col + HD], kn), cos, sin))
        q_g = jnp.concatenate(qs, axis=1)
        k_g = jnp.concatenate(ks, axis=1)
        v_g = va_ref[:, g * ATT_OUT:(g + 1) * ATT_OUT]
        kv_refs[g][:, 0:ATT_OUT] = k_g
        kv_refs[g][:, ATT_OUT:2 * ATT_OUT] = v_g
        qt = head_tiles(q_g, tile_q)
        kt = head_tiles(k_g, tile_k)
        vt = head_tiles(v_g, tile_v)

        if g == 2:
            n_slots = c_ref.shape[0]
            load = lambda jj, kv, n: c_ref[jj, :, kv]
            n_valid = lambda jj: nq
        else:
            n_slots = win // dil
            load = lambda jj, kv, n, dil=dil: c_ref[pl.ds(dil * jj, n), kv]
            n_valid = lambda jj, dil=dil, win=win: min(nq, win - dil * jj)
        n_main = max(jj + 1 for jj in range(n_slots) if n_valid(jj) == nq)
        new_keys = [list(range(i, -1, -dil)) for i in range(nq)]

        def update(state, scores, values):
            m, den, acc = state
            m_new = jnp.maximum(m, functools.reduce(jnp.maximum, scores))
            a = jnp.exp2(m - m_new)
            ps = [jnp.exp2(s - m_new) for s in scores]
            den = den * a + functools.reduce(lambda x, y: x + y, ps)
            acc = acc * a + functools.reduce(lambda x, y: x + y, [p * v for p, v in zip(ps, values)])
            return m_new, den, acc

        states = []
        for i in range(nq):
            t0 = new_keys[i][0]
            st = (_lane_sum(qt[i] * kt[t0]), jnp.ones((ATT_HEADS, HD), F32), vt[t0])
            if len(new_keys[i]) > 1:
                ts = new_keys[i][1:]
                st = update(st, [_lane_sum(qt[i] * kt[t]) for t in ts], [vt[t] for t in ts])
            states.append(st)

        def block(slots, states):
            scores = [[] for _ in range(nq)]
            values = [[] for _ in range(nq)]
            for jj, valid in slots:
                kt8 = load(jj, 0, valid)
                vt8 = load(jj, 1, valid)
                for i in range(valid):
                    scores[i].append(_lane_sum(qt[i] * kt8[i]))
                    values[i].append(vt8[i])
            return tuple(update(states[i], scores[i], values[i]) if scores[i] else states[i]
                         for i in range(nq))

        n_blocks = n_main // SLOT_BLOCK
        states = lax.fori_loop(
            0, n_blocks,
            lambda bi, st: block([(bi * SLOT_BLOCK + j, nq) for j in range(SLOT_BLOCK)], st),
            tuple(states), unroll=BLOCK_UNROLL)
        rest = [(jj, n_valid(jj)) for jj in range(n_blocks * SLOT_BLOCK, n_slots)]
        for k in range(0, len(rest), SLOT_BLOCK):
            states = block(rest[k:k + SLOT_BLOCK], states)
        o_groups.append([acc / den for (_, den, acc) in states])
        lse_groups.append([m * math.log(2.0) + jnp.log(den) for (m, den, _) in states])

    for i in range(nq):
        l0, l1, l2 = lse_groups[0][i], lse_groups[1][i], lse_groups[2][i]
        mx = jnp.maximum(jnp.maximum(l0, l1), l2)
        e0, e1, e2 = jnp.exp(l0 - mx), jnp.exp(l1 - mx), jnp.exp(l2 - mx)
        tot = e0 + e1 + e2
        o_ref[i] = ((e0 / tot) * o_groups[0][i] + (e1 / tot) * o_groups[1][i]
                    + (e2 / tot) * o_groups[2][i])


def _attn_sample(proj, c0, c1, c2, cos, sin, qn, kn):
    b, nq, _ = proj.shape
    kvw = 2 * ATT_OUT
    small = lambda shape: pl.BlockSpec(shape, lambda bi: (0, 0))
    tail = (2, ATT_HEADS, HD)
    outs = pl.pallas_call(
        _attn_sample_kernel,
        grid=(b,),
        in_specs=[
            pl.BlockSpec((None, nq, ATT_W), lambda bi: (bi, 0, OFF_QA // ATT_W)),
            pl.BlockSpec((None, nq, ATT_W), lambda bi: (bi, 0, OFF_KA // ATT_W)),
            pl.BlockSpec((None, nq, ATT_W), lambda bi: (bi, 0, OFF_VA // ATT_W)),
            pl.BlockSpec((None, None, c0.shape[2]) + tail, lambda bi: (0, bi, 0, 0, 0, 0)),
            pl.BlockSpec((None, None, c1.shape[2]) + tail, lambda bi: (0, bi, 0, 0, 0, 0)),
            pl.BlockSpec((None, None, c2.shape[2], nq) + tail, lambda bi: (0, bi, 0, 0, 0, 0, 0)),
            small((nq, HD)), small((nq, HD)), small((1, HD)), small((1, HD)),
        ],
        out_specs=[pl.BlockSpec((None, nq, kvw), lambda bi: (bi, 0, 0))] * 3
        + [pl.BlockSpec((None, nq, ATT_HEADS, HD), lambda bi: (bi, 0, 0, 0))],
        out_shape=[jax.ShapeDtypeStruct((b, nq, kvw), F32)] * 3
        + [jax.ShapeDtypeStruct((b, nq, ATT_HEADS, HD), F32)],
        scratch_shapes=[pltpu.VMEM((nq * ATT_HEADS, HD), F32)] * 3,
        compiler_params=_cparams(("parallel",)),
        name="attn_sample",
    )(proj, proj, proj, c0, c1, c2, cos, sin, qn, kn)
    return outs[:3], outs[3]


def _ret_tables(c):
    log_g = jnp.log1p(-jnp.exp2(-5.0 - jnp.arange(RET_HEADS, dtype=F32)))
    ii = jnp.arange(c, dtype=F32)
    diff = ii[:, None] - ii[None, :]
    inner = jnp.where(diff[None] >= 0, jnp.exp(jnp.maximum(diff, 0.0)[None] * log_g[:, None, None]), 0.0)
    q_dec = jnp.exp((ii + 1.0)[None, :] * log_g[:, None])
    k_dec = jnp.exp((c - 1.0 - ii)[None, :] * log_g[:, None])
    chunk_dec = jnp.exp(c * log_g)
    return inner, q_dec, k_dec, chunk_dec


def _ret_prompt_kernel(q_ref, k_ref, v_ref, g_ref, cos_ref, sin_ref, inner_ref, qd_ref, kd_ref, cd_ref,
                       o_ref, st_ref, q_s, k_s, s_s):
    L = q_ref.shape[0]
    C = RET_CHUNK
    q_s[...] = _rope(q_ref[...], cos_ref[...], sin_ref[...])
    k_s[...] = _rope(k_ref[...], cos_ref[...], sin_ref[...]) * (RET_DK ** -0.5)
    s_s[...] = jnp.zeros(s_s.shape, F32)
    inner = inner_ref[...]
    qd = qd_ref[...]
    kd = kd_ref[...]
    cd = cd_ref[...]

    def body(ci, carry):
        sl = pl.ds(pl.multiple_of(ci * C, C), C)
        qc = q_s[sl, :]
        kc = k_s[sl, :]
        vc = v_ref[sl, :].astype(BF16)
        state = s_s[...]
        sc = lax.dot_general(qc.astype(BF16), kc.astype(BF16), (((1,), (1,)), ((), ())),
                             preferred_element_type=F32) * inner
        o = (jnp.dot(sc.astype(BF16), vc, preferred_element_type=F32)
             + jnp.dot((qc * qd).astype(BF16), state.astype(BF16), preferred_element_type=F32))
        kdt = jnp.transpose(kc * kd).astype(BF16)
        s_s[...] = state * cd + jnp.dot(kdt, vc, preferred_element_type=F32)
        ms = jnp.mean(o * o, axis=-1, keepdims=True)
        gate = g_ref[sl, :]
        o_ref[sl, :] = (o * lax.rsqrt(ms + EPS) * (gate * _sigmoid(gate))).astype(o_ref.dtype)
        return carry

    lax.fori_loop(0, L // C, body, 0)
    st_ref[...] = s_s[...]


def _ret_prompt(proj, cos, sin):
    b, l, _ = proj.shape
    inner, q_dec, k_dec, chunk_dec = _ret_tables(RET_CHUNK)
    qd = jnp.broadcast_to(q_dec[:, :, None], (RET_HEADS, RET_CHUNK, RET_DK))
    kd = jnp.broadcast_to(k_dec[:, :, None], (RET_HEADS, RET_CHUNK, RET_DK))
    cd = jnp.broadcast_to(chunk_dec[:, None, None], (RET_HEADS, RET_DK, RET_DV))
    per_head = lambda r, c: pl.BlockSpec((None, r, c), lambda bi, hi: (hi, 0, 0))
    return pl.pallas_call(
        _ret_prompt_kernel,
        grid=(b, RET_HEADS),
        in_specs=[
            pl.BlockSpec((None, l, RET_DK), lambda bi, hi: (bi, 0, OFF_QR // RET_DK + hi)),
            pl.BlockSpec((None, l, RET_DK), lambda bi, hi: (bi, 0, OFF_KR // RET_DK + hi)),
            pl.BlockSpec((None, l, RET_DV), lambda bi, hi: (bi, 0, OFF_VR // RET_DV + hi)),
            pl.BlockSpec((None, l, RET_DV), lambda bi, hi: (bi, 0, OFF_GR // RET_DV + hi)),
            pl.BlockSpec((l, RET_DK), lambda bi, hi: (0, 0)),
            pl.BlockSpec((l, RET_DK), lambda bi, hi: (0, 0)),
            per_head(RET_CHUNK, RET_CHUNK), per_head(RET_CHUNK, RET_DK), per_head(RET_CHUNK, RET_DK),
            per_head(RET_DK, RET_DV),
        ],
        out_specs=[
            pl.BlockSpec((None, l, RET_DV), lambda bi, hi: (bi, 0, hi)),
            pl.BlockSpec((None, None, RET_DK, RET_DV), lambda bi, hi: (bi, hi, 0, 0)),
        ],
        out_shape=[jax.ShapeDtypeStruct((b, l, RET_V_W), BF16),
                   jax.ShapeDtypeStruct((b, RET_HEADS, RET_DK, RET_DV), F32)],
        scratch_shapes=[pltpu.VMEM((l, RET_DK), F32), pltpu.VMEM((l, RET_DK), F32),
                        pltpu.VMEM((RET_DK, RET_DV), F32)],
        compiler_params=_cparams(("parallel", "arbitrary")),
        name="ret_prompt",
    )(proj, proj, proj, proj, cos, sin, inner, qd, kd, cd)


def _ret_sample_kernel(a_ref, b_ref, s0_ref, cos_ref, sin_ref, inner_ref, qd_ref, kd_ref, cd_ref,
                       o_ref, st_ref):
    nq = a_ref.shape[0]
    cos = cos_ref[...]
    sin = sin_ref[...]
    pad = 128 - nq
    for h in range(RET_HEADS):
        q = _rope(a_ref[:, h * RET_DK:(h + 1) * RET_DK], cos, sin)
        k = _rope(a_ref[:, RET_QK_W + h * RET_DK:RET_QK_W + (h + 1) * RET_DK], cos, sin) * (RET_DK ** -0.5)
        vc0 = h * RET_DV
        if vc0 < ATT_OUT:
            v = a_ref[:, 2 * RET_QK_W + vc0:2 * RET_QK_W + vc0 + RET_DV]
        else:
            v = b_ref[:, vc0 - ATT_OUT:vc0 - ATT_OUT + RET_DV]
        gate = b_ref[:, ATT_OUT + h * RET_DV:ATT_OUT + (h + 1) * RET_DV]
        state = s0_ref[h]
        kp = jnp.concatenate([k, jnp.zeros((pad, RET_DK), F32)], axis=0)
        vp = jnp.concatenate([v, jnp.zeros((pad, RET_DV), F32)], axis=0).astype(BF16)
        sc = lax.dot_general(q.astype(BF16), kp.astype(BF16), (((1,), (1,)), ((), ())),
                             preferred_element_type=F32) * inner_ref[h]
        o = (jnp.dot(sc.astype(BF16), vp, preferred_element_type=F32)
             + jnp.dot((q * qd_ref[h]).astype(BF16), state.astype(BF16), preferred_element_type=F32))
        kdt = jnp.transpose(kp * kd_ref[h]).astype(BF16)
        st_ref[h] = state * cd_ref[h] + jnp.dot(kdt, vp, preferred_element_type=F32)
        ms = jnp.mean(o * o, axis=-1, keepdims=True)
        o_ref[:, h * RET_DV:(h + 1) * RET_DV] = o * lax.rsqrt(ms + EPS) * (gate * _sigmoid(gate))


def _ret_sample(proj, state, cos, sin):
    b, nq, _ = proj.shape
    inner, q_dec, k_dec, chunk_dec = _ret_tables(math.gcd(nq, RET_CHUNK))
    pad = 128 - nq
    inner_p = jnp.pad(inner, ((0, 0), (0, 0), (0, pad)))
    qd = jnp.broadcast_to(q_dec[:, :, None], (RET_HEADS, nq, RET_DK))
    kd = jnp.broadcast_to(jnp.pad(k_dec, ((0, 0), (0, pad)))[:, :, None], (RET_HEADS, 128, RET_DK))
    cd = jnp.broadcast_to(chunk_dec[:, None, None], (RET_HEADS, RET_DK, RET_DV))
    full = lambda a: pl.BlockSpec(a.shape, lambda bi: (0,) * a.ndim)
    return pl.pallas_call(
        _ret_sample_kernel,
        grid=(b,),
        in_specs=[
            pl.BlockSpec((None, nq, ATT_W), lambda bi: (bi, 0, OFF_QR // ATT_W)),
            pl.BlockSpec((None, nq, ATT_W), lambda bi: (bi, 0, OFF_QR // ATT_W + 1)),
            pl.BlockSpec((None, RET_HEADS, RET_DK, RET_DV), lambda bi: (bi, 0, 0, 0)),
            full(cos), full(sin), full(inner_p), full(qd), full(kd), full(cd),
        ],
        out_specs=[
            pl.BlockSpec((None, nq, RET_V_W), lambda bi: (bi, 0, 0)),
            pl.BlockSpec((None, RET_HEADS, RET_DK, RET_DV), lambda bi: (bi, 0, 0, 0)),
        ],
        out_shape=[jax.ShapeDtypeStruct((b, nq, RET_V_W), F32),
                   jax.ShapeDtypeStruct((b, RET_HEADS, RET_DK, RET_DV), F32)],
        compiler_params=_cparams(("parallel",)),
        name="ret_sample",
    )(proj, proj, state, cos, sin, inner_p, qd, kd, cd)


def _branch_merge_kernel(oa_ref, or_ref, ga_ref, gb_ref, wa_ref, wr_ref, o_ref):
    a = jnp.dot(oa_ref[...], wa_ref[...], preferred_element_type=F32)
    r = jnp.dot(or_ref[...], wr_ref[...], preferred_element_type=F32)
    o_ref[...] = (_sigmoid(ga_ref[...]) * a + _sigmoid(gb_ref[...]) * r).astype(o_ref.dtype)


def _branch_merge(o_att, o_ret, proj, w_att, w_ret, tm, tn):
    t = o_att.shape[0]
    n = w_att.shape[1]
    tm = min(tm, t)
    return pl.pallas_call(
        _branch_merge_kernel,
        grid=(t // tm, n // tn),
        in_specs=[
            pl.BlockSpec((tm, o_att.shape[1]), lambda i, j: (i, 0)),
            pl.BlockSpec((tm, o_ret.shape[1]), lambda i, j: (i, 0)),
            pl.BlockSpec((tm, tn), lambda i, j: (i, OFF_GA // tn + j)),
            pl.BlockSpec((tm, tn), lambda i, j: (i, (OFF_GA + n) // tn + j)),
            pl.BlockSpec((w_att.shape[0], tn), lambda i, j: (0, j)),
            pl.BlockSpec((w_ret.shape[0], tn), lambda i, j: (0, j)),
        ],
        out_specs=pl.BlockSpec((tm, tn), lambda i, j: (i, j)),
        out_shape=jax.ShapeDtypeStruct((t, n), BF16),
        compiler_params=_cparams(("parallel", "arbitrary")),
        name="branch_merge",
    )(o_att, o_ret, proj, proj, w_att, w_ret)


def _residual_matmul_kernel(x_ref, m_ref, w_ref, o_ref):
    o_ref[...] = x_ref[...] + jnp.dot(m_ref[...], w_ref[...], preferred_element_type=F32)


def _residual_matmul(x, mixed, w, tm, tn):
    t, d = x.shape
    tm = min(tm, t)
    return pl.pallas_call(
        _residual_matmul_kernel,
        grid=(t // tm, d // tn),
        in_specs=[
            pl.BlockSpec((tm, tn), lambda i, j: (i, j)),
            pl.BlockSpec((tm, mixed.shape[1]), lambda i, j: (i, 0)),
            pl.BlockSpec((w.shape[0], tn), lambda i, j: (0, j)),
        ],
        out_specs=pl.BlockSpec((tm, tn), lambda i, j: (i, j)),
        out_shape=jax.ShapeDtypeStruct((t, d), F32),
        compiler_params=_cparams(("parallel", "arbitrary")),
        name="mix_out_residual",
    )(x, mixed, w)


def _extract_top(x, n):
    tops = []
    for _ in range(n):
        m = jnp.max(x, axis=0, keepdims=True)
        tops.append(m)
        x = jnp.where(x == m, F32_MIN, x)
    return tops


def _peer_route_kernel(h_ref, nw_ref, wq_ref, sk_ref, hn_ref, s1_ref, s2_ref, e1_ref, e2_ref, tau_ref):
    x = h_ref[...]
    ms = jnp.mean(x * x, axis=-1, keepdims=True)
    hn = (x * lax.rsqrt(ms + EPS) * nw_ref[...]).astype(BF16)
    hn_ref[...] = hn
    q = jnp.dot(hn, wq_ref[...], preferred_element_type=F32).astype(BF16)
    half = PEER_NKEYS
    for h in range(PEER_HEADS):
        tops = []
        for p in range(2):
            hp = 2 * h + p
            s_t = lax.dot_general(sk_ref[hp], q[:, hp * half:(hp + 1) * half], (((1,), (1,)), ((), ())),
                                  preferred_element_type=F32)
            top = _extract_top(s_t, PEER_TOPK + 1)
            tops.append(top)
            e = jnp.exp(s_t - top[0])
            if p == 0:
                s1_ref[h] = s_t
                e1_ref[h] = e
            else:
                s2_ref[h] = s_t
                e2_raw = e
        t2 = jnp.concatenate(tops[1][:PEER_TOPK], axis=0)
        t2_head = jnp.concatenate(tops[1][:8], axis=0)
        cand = jnp.concatenate([tops[0][0] + t2] + [tops[0][a] + t2_head for a in range(1, PEER_TOPK)],
                               axis=0)
        best = _extract_top(cand, PEER_TOPK + 1)
        z = jnp.zeros_like(best[0])
        for c in best[:PEER_TOPK]:
            z = z + jnp.exp(c - best[0])
        e2_ref[h] = e2_raw / z
        runner_up = jnp.maximum(best[PEER_TOPK], jnp.maximum(tops[0][PEER_TOPK] + tops[1][0],
                                                             tops[0][0] + tops[1][PEER_TOPK]))
        tau_ref[pl.ds(h, 1), :] = 0.5 * (best[PEER_TOPK - 1] + runner_up)


def _peer_route(h, norm_w, wq, sub_keys, tm):
    t, d = h.shape
    tm = min(tm, t)
    stat = jax.ShapeDtypeStruct((PEER_HEADS, PEER_NKEYS, t), F32)
    stat_spec = pl.BlockSpec((PEER_HEADS, PEER_NKEYS, tm), lambda i: (0, 0, i))
    return pl.pallas_call(
        _peer_route_kernel,
        grid=(t // tm,),
        in_specs=[
            pl.BlockSpec((tm, d), lambda i: (i, 0)),
            pl.BlockSpec((1, d), lambda i: (0, 0)),
            pl.BlockSpec(wq.shape, lambda i: (0, 0)),
            pl.BlockSpec(sub_keys.shape, lambda i: (0, 0, 0)),
        ],
        out_specs=[pl.BlockSpec((tm, d), lambda i: (i, 0))] + [stat_spec] * 4
        + [pl.BlockSpec((PEER_HEADS, tm), lambda i: (0, i))],
        out_shape=[jax.ShapeDtypeStruct((t, d), BF16)] + [stat] * 4
        + [jax.ShapeDtypeStruct((PEER_HEADS, t), F32)],
        compiler_params=_cparams(("parallel",)),
        name="peer_route",
    )(h, norm_w.reshape(1, d), wq, sub_keys)


def _gelu_tanh(x):
    c = math.sqrt(2.0 / math.pi)
    inner = x * (c + (c * 0.044715) * (x * x))
    half = 0.5 * x
    return half + half * jnp.tanh(inner)


def _peer_expert_kernel(hn_ref, u_ref, v_ref, s1_ref, e1_ref, s2_ref, e2_ref, tau_ref, h_ref, y_ref,
                        act_s, g_s):
    e_tile = u_ref.shape[0]
    tm = hn_ref.shape[0]
    nk = PEER_NKEYS
    lc = min(LANE_CHUNK, tm)

    @pl.when(pl.program_id(1) == 0)
    def _():
        y_ref[...] = h_ref[...]

    act_s[...] = lax.dot_general(u_ref[...], hn_ref[...], (((1,), (1,)), ((), ())),
                                 preferred_element_type=F32)

    def body(a, carry):
        rows = pl.ds(pl.multiple_of(a * nk, nk), nk)
        thr = [tau_ref[h:h + 1, :] - s1_ref[h, pl.ds(a, 1), :] for h in range(PEER_HEADS)]
        e1rows = [e1_ref[h, pl.ds(a, 1), :] for h in range(PEER_HEADS)]
        for tcol in range(tm // lc):
            lanes = slice(tcol * lc, (tcol + 1) * lc)
            w = jnp.zeros((nk, lc), F32)
            for h in range(PEER_HEADS):
                gate = e2_ref[h, :, lanes] * e1rows[h][:, lanes]
                w = w + jnp.where(s2_ref[h, :, lanes] >= thr[h][:, lanes], gate, 0.0)
            g_s[rows, lanes] = (w * _gelu_tanh(act_s[rows, lanes])).astype(BF16)
        return carry

    lax.fori_loop(0, e_tile // nk, body, 0)
    y_ref[...] += lax.dot_general(g_s[...], v_ref[...], (((0,), (0,)), ((), ())),
                                  preferred_element_type=F32)


def _peer_expert(hn, u, v, s1, s2, e1, e2, tau, h, tm, e_tile):
    t, d = h.shape
    tm = min(tm, t)
    n_exp = u.shape[0]
    a_tile = e_tile // PEER_NKEYS
    row_spec = pl.BlockSpec((tm, d), lambda i, e: (i, 0))
    a_spec = pl.BlockSpec((PEER_HEADS, a_tile, tm), lambda i, e: (0, e, i))
    b_spec = pl.BlockSpec((PEER_HEADS, PEER_NKEYS, tm), lambda i, e: (0, 0, i))
    return pl.pallas_call(
        _peer_expert_kernel,
        grid=(t // tm, n_exp // e_tile),
        in_specs=[
            row_spec,
            pl.BlockSpec((e_tile, d), lambda i, e: (e, 0)),
            pl.BlockSpec((e_tile, d), lambda i, e: (e, 0)),
            a_spec, a_spec, b_spec, b_spec,
            pl.BlockSpec((PEER_HEADS, tm), lambda i, e: (0, i)),
            row_spec,
        ],
        out_specs=row_spec,
        out_shape=jax.ShapeDtypeStruct((t, d), F32),
        scratch_shapes=[pltpu.VMEM((e_tile, tm), F32), pltpu.VMEM((e_tile, tm), BF16)],
        compiler_params=_cparams(("parallel", "arbitrary")),
        name="peer_expert",
    )(hn, u, v, s1, e1, s2, e2, tau, h)


def _token_tail(x, proj, o_att, o_ret, w, tm):
    mixed = _branch_merge(o_att, o_ret, proj, w["att_br"], w["ret_br"], tm, 512)
    h = _residual_matmul(x, mixed, w["mix_out"], tm, 512)
    hn, s1, s2, e1, e2, tau = _peer_route(h, w["norm2"], w["peer_q"], w["sub_keys"], 256)
    return _peer_expert(hn, w["peer_u"], w["peer_v"], s1, s2, e1, e2, tau, h, 512, 1024)


def kernel(x_prompt, x_sample, cache_att_w128, cache_att_w512, cache_att_w2048, state_ret, norm1_w, w_in,
           att_q_norm_w, att_k_norm_w, w_att_br, w_ret_br, w_mix_out, norm2_w, peer_w_q, peer_sub_keys,
           peer_u, peer_v):
    depth = w_in.shape[0]
    assert depth == 1, "single-layer trunk"
    b, l, d = x_prompt.shape
    sb, sl, _ = x_sample.shape
    past = cache_att_w2048.shape[2]
    w = {
        "att_br": w_att_br[0].astype(BF16),
        "ret_br": w_ret_br[0].astype(BF16),
        "mix_out": w_mix_out[0].astype(BF16),
        "norm2": norm2_w[0],
        "peer_q": peer_w_q[0].astype(BF16),
        "sub_keys": peer_sub_keys[0].reshape(PEER_HEADS * 2, PEER_NKEYS, -1).astype(BF16),
        "peer_u": peer_u[0].astype(BF16),
        "peer_v": peer_v[0].astype(BF16),
    }
    w_in_b = w_in[0].astype(BF16)
    qn = att_q_norm_w[0].reshape(1, HD)
    kn = att_k_norm_w[0].reshape(1, HD)
    xp = x_prompt.reshape(b * l, d)
    xs = x_sample.reshape(sb * sl, d)

    proj_p = _norm_matmul(xp, norm1_w[0], w_in_b, 1024, 1024)
    proj_s = _norm_matmul(xs, norm1_w[0], w_in_b, 1024, 1024)
    in_w = proj_p.shape[1]

    cos_p, sin_p = _rope_tables(jnp.arange(l, dtype=jnp.int32), HD)
    kv_p, o_att_p = _attn_prompt(proj_p.reshape(b, l, in_w), cos_p, sin_p, qn, kn)
    o_ret_p, st_p = _ret_prompt(proj_p.reshape(b, l, in_w), cos_p, sin_p)
    y_p = _token_tail(xp, proj_p, o_att_p.reshape(b * l, ATT_OUT), o_ret_p.reshape(b * l, RET_V_W), w, 1024)

    cos_s, sin_s = _rope_tables(past + jnp.arange(sl, dtype=jnp.int32), HD)
    kvw = 2 * ATT_OUT
    c2 = cache_att_w2048.reshape(1, sb, past // 16, 16, 2, ATT_HEADS, HD)
    kv_s, o_att_s = _attn_sample(proj_s.reshape(sb, sl, in_w), cache_att_w128, cache_att_w512, c2,
                                 cos_s, sin_s, qn, kn)
    o_ret_s, st_s = _ret_sample(proj_s.reshape(sb, sl, in_w), state_ret[0], cos_s, sin_s)
    y_s = _token_tail(xs, proj_s, o_att_s.reshape(sb * sl, ATT_OUT).astype(BF16),
                      o_ret_s.reshape(sb * sl, RET_V_W).astype(BF16), w, 1024)

    new_p = []
    for g, (win, _) in enumerate(ATT_GROUPS):
        rows = min(win, l)
        kk = kv_p[2 * g][:, l - rows:].reshape(b, rows, 1, ATT_HEADS, HD)
        vv = kv_p[2 * g + 1][:, l - rows:].reshape(b, rows, 1, ATT_HEADS, HD)
        new_p.append(jnp.concatenate([kk, vv], axis=2)[None])
    new_s = [kv.reshape(1, sb, sl, 2, ATT_HEADS, HD) for kv in kv_s]
    return (y_p.reshape(b, l, d), y_s.reshape(sb, sl, d),
            new_p[0], new_p[1], new_p[2], st_p[None],
            new_s[0], new_s[1], new_s[2], st_s[None])
```

```python
import functools
import math

import jax
import jax.numpy as jnp
from jax import lax
from jax.experimental import pallas as pl
from jax.experimental.pallas import tpu as pltpu

F32 = jnp.float32
BF16 = jnp.bfloat16

ATT_GROUPS = ((128, 1), (512, 4), (2048, 16))
N_GROUPS = 3
ATT_HEADS = 8
HD = 128
ATT_W = N_GROUPS * ATT_HEADS * HD
ATT_OUT = ATT_HEADS * HD
N_KEYS = 129
ROPE_THETA = 10000.0
RET_HEADS = 8
RET_DK = 128
RET_DV = 256
RET_QK_W = RET_HEADS * RET_DK
RET_V_W = RET_HEADS * RET_DV
RET_CHUNK = 128
PEER_HEADS = 8
PEER_NKEYS = 128
PEER_TOPK = 16
EPS = 1e-6
NEG = -1e30
F32_MIN = float(jnp.finfo(jnp.float32).min)

OFF_QA = 0
OFF_KA = ATT_W
OFF_VA = 2 * ATT_W
OFF_QR = 3 * ATT_W
OFF_KR = OFF_QR + RET_QK_W
OFF_VR = OFF_KR + RET_QK_W
OFF_GR = OFF_VR + RET_V_W
OFF_GA = OFF_GR + RET_V_W

RET_HEADS_PER_STEP = 2
BLOCK_UNROLL = 4
SLOT_BLOCK = 8
LANE_CHUNK = 256
VMEM_LIMIT = 56 * 1024 * 1024


def _cparams(sem):
    return pltpu.CompilerParams(dimension_semantics=sem, vmem_limit_bytes=VMEM_LIMIT)


def _sigmoid(x):
    return 1.0 / (1.0 + jnp.exp(-x))


def _rope_tables(pos, width):
    half = width // 2
    inv_freq = jnp.power(jnp.float32(ROPE_THETA), -jnp.arange(half, dtype=F32) / half)
    ang = pos.astype(F32)[:, None] * inv_freq[None, :]
    cos, sin = jnp.cos(ang), jnp.sin(ang)
    return jnp.concatenate([cos, cos], axis=-1), jnp.concatenate([-sin, sin], axis=-1)


def _rope(x, cos, sin):
    return x * cos + pltpu.roll(x, HD // 2, 1) * sin


def _head_rmsnorm(x, w):
    ms = jnp.mean(x * x, axis=-1, keepdims=True)
    return x * lax.rsqrt(ms + EPS) * w


def _norm_matmul_kernel(x_ref, nw_ref, w_ref, o_ref, xn_ref):
    @pl.when(pl.program_id(1) == 0)
    def _():
        x = x_ref[...]
        ms = jnp.mean(x * x, axis=-1, keepdims=True)
        xn_ref[...] = (x * lax.rsqrt(ms + EPS) * nw_ref[...]).astype(BF16)

    o_ref[...] = jnp.dot(xn_ref[...], w_ref[...], preferred_element_type=F32)


def _norm_matmul(x, norm_w, w_bf16, tm, tn):
    t, d = x.shape
    n = w_bf16.shape[1]
    tm = min(tm, t)
    return pl.pallas_call(
        _norm_matmul_kernel,
        grid=(t // tm, n // tn),
        in_specs=[
            pl.BlockSpec((tm, d), lambda i, j: (i, 0)),
            pl.BlockSpec((1, d), lambda i, j: (0, 0)),
            pl.BlockSpec((d, tn), lambda i, j: (0, j)),
        ],
        out_specs=pl.BlockSpec((tm, tn), lambda i, j: (i, j)),
        out_shape=jax.ShapeDtypeStruct((t, n), F32),
        scratch_shapes=[pltpu.VMEM((tm, d), BF16)],
        compiler_params=_cparams(("parallel", "arbitrary")),
        name="norm_inproj",
    )(x, norm_w.reshape(1, d), w_bf16)


def _attn_prompt_kernel(q0, k0, v0, q1, k1, v1, q2, k2, v2, cos_ref, sin_ref, qn_ref, kn_ref,
                        ko0, vo0, ko1, vo1, ko2, vo2, o_ref,
                        q_s, k_s, v_s, og_s, lse_s):
    L = q0.shape[0]
    blk = 128
    scale = HD ** -0.5
    cos = cos_ref[...]
    sin = sin_ref[...]
    k_s[pl.ds(0, L), :] = jnp.zeros((L, HD), F32)
    v_s[pl.ds(0, L), :] = jnp.zeros((L, HD), F32)
    groups = ((q0, k0, v0, ko0, vo0), (q1, k1, v1, ko1, vo1), (q2, k2, v2, ko2, vo2))
    for g, (qr, kr, vr, ko, vo) in enumerate(groups):
        win, dil = ATT_GROUPS[g]
        q = _rope(_head_rmsnorm(qr[...], qn_ref[...]), cos, sin) * scale
        k = _rope(_head_rmsnorm(kr[...], kn_ref[...]), cos, sin)
        v = vr[...]
        ko[...] = k
        vo[...] = v
        q_s[...] = q
        k_s[pl.ds(L, L), :] = k
        v_s[pl.ds(L, L), :] = v
        log_d = int(math.log2(dil))

        def body(t, carry, dil=dil, log_d=log_d, g=g):
            r = jnp.bitwise_and(t, dil - 1)
            i = lax.shift_right_logical(t, log_d)
            qstart = r + dil * blk * i
            kstart = L + qstart - dil * blk
            if dil == 1:
                qsl = pl.ds(qstart, blk)
                ksl = pl.ds(kstart, 2 * blk)
            else:
                qsl = pl.ds(qstart, blk, stride=dil)
                ksl = pl.ds(kstart, 2 * blk, stride=dil)
            qb = q_s[qsl, :].astype(BF16)
            kb = k_s[ksl, :].astype(BF16)
            vb = v_s[ksl, :].astype(BF16)
            s = lax.dot_general(qb, kb, (((1,), (1,)), ((), ())), preferred_element_type=F32)
            qi = lax.broadcasted_iota(jnp.int32, (blk, 2 * blk), 0)
            kj = lax.broadcasted_iota(jnp.int32, (blk, 2 * blk), 1)
            lo = jnp.maximum(qi, jnp.where(i == 0, blk, 0))
            valid = jnp.logical_and(kj >= lo, kj <= qi + blk)
            s = jnp.where(valid, s, NEG)
            m = jnp.max(s, axis=-1, keepdims=True)
            p = jnp.exp(s - m)
            den = jnp.sum(p, axis=-1, keepdims=True)
            o = jnp.dot((p / den).astype(BF16), vb, preferred_element_type=F32)
            lse = m + jnp.log(den)
            osl = pl.ds(g * L + qstart, blk) if dil == 1 else pl.ds(g * L + qstart, blk, stride=dil)
            og_s[osl, :] = o
            lse_s[osl, :] = jnp.broadcast_to(lse, (blk, HD))
            return carry

        lax.fori_loop(0, L // blk, body, 0, unroll=True)

    l0 = lse_s[pl.ds(0, L), :]
    l1 = lse_s[pl.ds(L, L), :]
    l2 = lse_s[pl.ds(2 * L, L), :]
    mx = jnp.maximum(jnp.maximum(l0, l1), l2)
    e0 = jnp.exp(l0 - mx)
    e1 = jnp.exp(l1 - mx)
    e2 = jnp.exp(l2 - mx)
    tot = e0 + e1 + e2
    o = ((e0 / tot) * og_s[pl.ds(0, L), :] + (e1 / tot) * og_s[pl.ds(L, L), :]
         + (e2 / tot) * og_s[pl.ds(2 * L, L), :])
    o_ref[...] = o.astype(o_ref.dtype)


def _attn_prompt(proj, cos, sin, qn, kn):
    b, l, _ = proj.shape
    in_specs = []
    for g in range(N_GROUPS):
        for off in (OFF_QA, OFF_KA, OFF_VA):
            cb = (off + g * ATT_OUT) // HD
            in_specs.append(pl.BlockSpec((None, l, HD), lambda bi, hi, cb=cb: (bi, 0, cb + hi)))
    in_specs += [
        pl.BlockSpec((l, HD), lambda bi, hi: (0, 0)),
        pl.BlockSpec((l, HD), lambda bi, hi: (0, 0)),
        pl.BlockSpec((1, HD), lambda bi, hi: (0, 0)),
        pl.BlockSpec((1, HD), lambda bi, hi: (0, 0)),
    ]
    head_spec = pl.BlockSpec((None, l, HD), lambda bi, hi: (bi, 0, hi))
    kv_shape = jax.ShapeDtypeStruct((b, l, ATT_OUT), F32)
    outs = pl.pallas_call(
        _attn_prompt_kernel,
        grid=(b, ATT_HEADS),
        in_specs=in_specs,
        out_specs=[head_spec] * 7,
        out_shape=[kv_shape] * 6 + [jax.ShapeDtypeStruct((b, l, ATT_OUT), BF16)],
        scratch_shapes=[
            pltpu.VMEM((l, HD), F32),
            pltpu.VMEM((2 * l, HD), F32),
            pltpu.VMEM((2 * l, HD), F32),
            pltpu.VMEM((N_GROUPS * l, HD), F32),
            pltpu.VMEM((N_GROUPS * l, HD), F32),
        ],
        compiler_params=_cparams(("parallel", "arbitrary")),
        name="attn_prompt",
    )(*([proj] * 9), cos, sin, qn, kn)
    return outs[:6], outs[6]


def _lane_sum(x):
    return jnp.broadcast_to(jnp.sum(x, axis=-1, keepdims=True), x.shape)


def _attn_sample_kernel(qa_ref, ka_ref, va_ref, c0_ref, c1_ref, c2_ref, cos_ref, sin_ref,
                        qn_ref, kn_ref, kv0_ref, kv1_ref, kv2_ref, o_ref, tile_q, tile_k, tile_v):
    nq = qa_ref.shape[0]
    scale = HD ** -0.5 * math.log2(math.e)
    cos = cos_ref[...]
    sin = sin_ref[...]
    qn = qn_ref[...]
    kn = kn_ref[...]
    caches = (c0_ref, c1_ref, c2_ref)
    kv_refs = (kv0_ref, kv1_ref, kv2_ref)

    def head_tiles(x, tile_s):
        for h in range(ATT_HEADS):
            tile_s[pl.ds(h, nq, stride=ATT_HEADS), :] = x[:, h * HD:(h + 1) * HD]
        return [tile_s[i * ATT_HEADS:(i + 1) * ATT_HEADS, :] for i in range(nq)]

    o_groups = []
    lse_groups = []
    for g in range(N_GROUPS):
        win, dil = ATT_GROUPS[g]
        c_ref = caches[g]
        qs, ks = [], []
        for h in range(ATT_HEADS):
            col = g * ATT_OUT + h * HD
            qs.append(_rope(_head_rmsnorm(qa_ref[:, col:col + HD], qn), cos, sin) * scale)
            ks.append(_rope(_head_rmsnorm(ka_ref[:, col:col + HD], kn), cos, sin))
        q_g = jnp.concatenate(qs, axis=1)
        k_g = jnp.concatenate(ks, axis=1)
        v_g = va_ref[:, g * ATT_OUT:(g + 1) * ATT_OUT]
        kv_refs[g][:, 0:ATT_OUT] = k_g
        kv_refs[g][:, ATT_OUT:2 * ATT_OUT] = v_g
        qt = head_tiles(q_g, tile_q)
        kt = head_tiles(k_g, tile_k)
        vt = head_tiles(v_g, tile_v)

        if g == 2:
            n_slots = c_ref.shape[0]
            load = lambda jj, kv, n: c_ref[jj, :, kv]
            n_valid = lambda jj: nq
        else:
            n_slots = win // dil
            load = lambda jj, kv, n, dil=dil: c_ref[pl.ds(dil * jj, n), kv]
            n_valid = lambda jj, dil=dil, win=win: min(nq, win - dil * jj)
        n_main = max(jj + 1 for jj in range(n_slots) if n_valid(jj) == nq)
        new_keys = [list(range(i, -1, -dil)) for i in range(nq)]

        def update(state, scores, values):
            m, den, acc = state
            m_new = jnp.maximum(m, functools.reduce(jnp.maximum, scores))
            a = jnp.exp2(m - m_new)
            ps = [jnp.exp2(s - m_new) for s in scores]
            den = den * a + functools.reduce(lambda x, y: x + y, ps)
            acc = acc * a + functools.reduce(lambda x, y: x + y, [p * v for p, v in zip(ps, values)])
            return m_new, den, acc

        states = []
        for i in range(nq):
            t0 = new_keys[i][0]
            st = (_lane_sum(qt[i] * kt[t0]), jnp.ones((ATT_HEADS, HD), F32), vt[t0])
            if len(new_keys[i]) > 1:
                ts = new_keys[i][1:]
                st = update(st, [_lane_sum(qt[i] * kt[t]) for t in ts], [vt[t] for t in ts])
            states.append(st)

        def block(slots, states):
            scores = [[] for _ in range(nq)]
            values = [[] for _ in range(nq)]
            for jj, valid in slots:
                kt8 = load(jj, 0, valid)
                vt8 = load(jj, 1, valid)
                for i in range(valid):
                    scores[i].append(_lane_sum(qt[i] * kt8[i]))
                    values[i].append(vt8[i])
            return tuple(update(states[i], scores[i], values[i]) if scores[i] else states[i]
                         for i in range(nq))

        n_blocks = n_main // SLOT_BLOCK
        states = lax.fori_loop(
            0, n_blocks,
            lambda bi, st: block([(bi * SLOT_BLOCK + j, nq) for j in range(SLOT_BLOCK)], st),
            tuple(states), unroll=BLOCK_UNROLL)
        rest = [(jj, n_valid(jj)) for jj in range(n_blocks * SLOT_BLOCK, n_slots)]
        for k in range(0, len(rest), SLOT_BLOCK):
            states = block(rest[k:k + SLOT_BLOCK], states)
        o_groups.append([acc / den for (_, den, acc) in states])
        lse_groups.append([m * math.log(2.0) + jnp.log(den) for (m, den, _) in states])

    for i in range(nq):
        l0, l1, l2 = lse_groups[0][i], lse_groups[1][i], lse_groups[2][i]
        mx = jnp.maximum(jnp.maximum(l0, l1), l2)
        e0, e1, e2 = jnp.exp(l0 - mx), jnp.exp(l1 - mx), jnp.exp(l2 - mx)
        tot = e0 + e1 + e2
        o_ref[i] = ((e0 / tot) * o_groups[0][i] + (e1 / tot) * o_groups[1][i]
                    + (e2 / tot) * o_groups[2][i])


def _attn_sample(proj, c0, c1, c2, cos, sin, qn, kn):
    b, nq, _ = proj.shape
    kvw = 2 * ATT_OUT
    small = lambda shape: pl.BlockSpec(shape, lambda bi: (0, 0))
    tail = (2, ATT_HEADS, HD)
    outs = pl.pallas_call(
        _attn_sample_kernel,
        grid=(b,),
        in_specs=[
            pl.BlockSpec((None, nq, ATT_W), lambda bi: (bi, 0, OFF_QA // ATT_W)),
            pl.BlockSpec((None, nq, ATT_W), lambda bi: (bi, 0, OFF_KA // ATT_W)),
            pl.BlockSpec((None, nq, ATT_W), lambda bi: (bi, 0, OFF_VA // ATT_W)),
            pl.BlockSpec((None, None, c0.shape[2]) + tail, lambda bi: (0, bi, 0, 0, 0, 0)),
            pl.BlockSpec((None, None, c1.shape[2]) + tail, lambda bi: (0, bi, 0, 0, 0, 0)),
            pl.BlockSpec((None, None, c2.shape[2], nq) + tail, lambda bi: (0, bi, 0, 0, 0, 0, 0)),
            small((nq, HD)), small((nq, HD)), small((1, HD)), small((1, HD)),
        ],
        out_specs=[pl.BlockSpec((None, nq, kvw), lambda bi: (bi, 0, 0))] * 3
        + [pl.BlockSpec((None, nq, ATT_HEADS, HD), lambda bi: (bi, 0, 0, 0))],
        out_shape=[jax.ShapeDtypeStruct((b, nq, kvw), F32)] * 3
        + [jax.ShapeDtypeStruct((b, nq, ATT_HEADS, HD), F32)],
        scratch_shapes=[pltpu.VMEM((nq * ATT_HEADS, HD), F32)] * 3,
        compiler_params=_cparams(("parallel",)),
        name="attn_sample",
    )(proj, proj, proj, c0, c1, c2, cos, sin, qn, kn)
    return outs[:3], outs[3]


def _ret_tables(c):
    log_g = jnp.log1p(-jnp.exp2(-5.0 - jnp.arange(RET_HEADS, dtype=F32)))
    ii = jnp.arange(c, dtype=F32)
    diff = ii[:, None] - ii[None, :]
    inner = jnp.where(diff[None] >= 0, jnp.exp(jnp.maximum(diff, 0.0)[None] * log_g[:, None, None]), 0.0)
    q_dec = jnp.exp((ii + 1.0)[None, :] * log_g[:, None])
    k_dec = jnp.exp((c - 1.0 - ii)[None, :] * log_g[:, None])
    chunk_dec = jnp.exp(c * log_g)
    return inner, q_dec, k_dec, chunk_dec


def _ret_prompt_kernel(q_ref, k_ref, v_ref, g_ref, cos_ref, sin_ref, inner_ref, qd_ref, kd_ref, cd_ref,
                       o_ref, st_ref, q_s, k_s, s_s):
    L = q_ref.shape[0]
    C = RET_CHUNK
    nh = RET_HEADS_PER_STEP
    cos = cos_ref[...]
    sin = sin_ref[...]
    for hh in range(nh):
        qk = slice(hh * RET_DK, (hh + 1) * RET_DK)
        q_s[:, qk] = _rope(q_ref[:, qk], cos, sin)
        k_s[:, qk] = _rope(k_ref[:, qk], cos, sin) * (RET_DK ** -0.5)
    s_s[...] = jnp.zeros(s_s.shape, F32)

    def body(ci, carry):
        sl = pl.ds(pl.multiple_of(ci * C, C), C)
        for hh in range(nh):
            qk = slice(hh * RET_DK, (hh + 1) * RET_DK)
            vv = slice(hh * RET_DV, (hh + 1) * RET_DV)
            qc = q_s[sl, qk]
            kc = k_s[sl, qk]
            vc = v_ref[sl, vv].astype(BF16)
            state = s_s[hh]
            sc = lax.dot_general(qc.astype(BF16), kc.astype(BF16), (((1,), (1,)), ((), ())),
                                 preferred_element_type=F32) * inner_ref[hh]
            o = (jnp.dot(sc.astype(BF16), vc, preferred_element_type=F32)
                 + jnp.dot((qc * qd_ref[hh]).astype(BF16), state.astype(BF16), preferred_element_type=F32))
            kdt = jnp.transpose(kc * kd_ref[hh]).astype(BF16)
            s_s[hh] = state * cd_ref[hh] + jnp.dot(kdt, vc, preferred_element_type=F32)
            ms = jnp.mean(o * o, axis=-1, keepdims=True)
            gate = g_ref[sl, vv]
            o_ref[sl, vv] = (o * lax.rsqrt(ms + EPS) * (gate * _sigmoid(gate))).astype(o_ref.dtype)
        return carry

    lax.fori_loop(0, L // C, body, 0)
    st_ref[...] = s_s[...]


def _ret_prompt(proj, cos, sin):
    b, l, _ = proj.shape
    nh = RET_HEADS_PER_STEP
    inner, q_dec, k_dec, chunk_dec = _ret_tables(RET_CHUNK)
    qd = jnp.broadcast_to(q_dec[:, :, None], (RET_HEADS, RET_CHUNK, RET_DK))
    kd = jnp.broadcast_to(k_dec[:, :, None], (RET_HEADS, RET_CHUNK, RET_DK))
    cd = jnp.broadcast_to(chunk_dec[:, None, None], (RET_HEADS, RET_DK, RET_DV))
    per_head = lambda r, c: pl.BlockSpec((nh, r, c), lambda bi, hi: (hi, 0, 0))
    qk_w, v_w = nh * RET_DK, nh * RET_DV
    return pl.pallas_call(
        _ret_prompt_kernel,
        grid=(b, RET_HEADS // nh),
        in_specs=[
            pl.BlockSpec((None, l, qk_w), lambda bi, hi: (bi, 0, OFF_QR // qk_w + hi)),
            pl.BlockSpec((None, l, qk_w), lambda bi, hi: (bi, 0, OFF_KR // qk_w + hi)),
            pl.BlockSpec((None, l, v_w), lambda bi, hi: (bi, 0, OFF_VR // v_w + hi)),
            pl.BlockSpec((None, l, v_w), lambda bi, hi: (bi, 0, OFF_GR // v_w + hi)),
            pl.BlockSpec((l, RET_DK), lambda bi, hi: (0, 0)),
            pl.BlockSpec((l, RET_DK), lambda bi, hi: (0, 0)),
            per_head(RET_CHUNK, RET_CHUNK), per_head(RET_CHUNK, RET_DK), per_head(RET_CHUNK, RET_DK),
            per_head(RET_DK, RET_DV),
        ],
        out_specs=[
            pl.BlockSpec((None, l, v_w), lambda bi, hi: (bi, 0, hi)),
            pl.BlockSpec((None, nh, RET_DK, RET_DV), lambda bi, hi: (bi, hi, 0, 0)),
        ],
        out_shape=[jax.ShapeDtypeStruct((b, l, RET_V_W), BF16),
                   jax.ShapeDtypeStruct((b, RET_HEADS, RET_DK, RET_DV), F32)],
        scratch_shapes=[pltpu.VMEM((l, qk_w), F32), pltpu.VMEM((l, qk_w), F32),
                        pltpu.VMEM((nh, RET_DK, RET_DV), F32)],
        compiler_params=_cparams(("parallel", "arbitrary")),
        name="ret_prompt",
    )(proj, proj, proj, proj, cos, sin, inner, qd, kd, cd)


def _ret_sample_kernel(a_ref, b_ref, s0_ref, cos_ref, sin_ref, inner_ref, qd_ref, kd_ref, cd_ref,
                       o_ref, st_ref):
    nq = a_ref.shape[0]
    cos = cos_ref[...]
    sin = sin_ref[...]
    pad = 128 - nq
    for h in range(RET_HEADS):
        q = _rope(a_ref[:, h * RET_DK:(h + 1) * RET_DK], cos, sin)
        k = _rope(a_ref[:, RET_QK_W + h * RET_DK:RET_QK_W + (h + 1) * RET_DK], cos, sin) * (RET_DK ** -0.5)
        vc0 = h * RET_DV
        if vc0 < ATT_OUT:
            v = a_ref[:, 2 * RET_QK_W + vc0:2 * RET_QK_W + vc0 + RET_DV]
        else:
            v = b_ref[:, vc0 - ATT_OUT:vc0 - ATT_OUT + RET_DV]
        gate = b_ref[:, ATT_OUT + h * RET_DV:ATT_OUT + (h + 1) * RET_DV]
        state = s0_ref[h]
        kp = jnp.concatenate([k, jnp.zeros((pad, RET_DK), F32)], axis=0)
        vp = jnp.concatenate([v, jnp.zeros((pad, RET_DV), F32)], axis=0).astype(BF16)
        sc = lax.dot_general(q.astype(BF16), kp.astype(BF16), (((1,), (1,)), ((), ())),
                             preferred_element_type=F32) * inner_ref[h]
        o = (jnp.dot(sc.astype(BF16), vp, preferred_element_type=F32)
             + jnp.dot((q * qd_ref[h]).astype(BF16), state.astype(BF16), preferred_element_type=F32))
        kdt = jnp.transpose(kp * kd_ref[h]).astype(BF16)
        st_ref[h] = state * cd_ref[h] + jnp.dot(kdt, vp, preferred_element_type=F32)
        ms = jnp.mean(o * o, axis=-1, keepdims=True)
        o_ref[:, h * RET_DV:(h + 1) * RET_DV] = o * lax.rsqrt(ms + EPS) * (gate * _sigmoid(gate))


def _ret_sample(proj, state, cos, sin):
    b, nq, _ = proj.shape
    inner, q_dec, k_dec, chunk_dec = _ret_tables(math.gcd(nq, RET_CHUNK))
    pad = 128 - nq
    inner_p = jnp.pad(inner, ((0, 0), (0, 0), (0, pad)))
    qd = jnp.broadcast_to(q_dec[:, :, None], (RET_HEADS, nq, RET_DK))
    kd = jnp.broadcast_to(jnp.pad(k_dec, ((0, 0), (0, pad)))[:, :, None], (RET_HEADS, 128, RET_DK))
    cd = jnp.broadcast_to(chunk_dec[:, None, None], (RET_HEADS, RET_DK, RET_DV))
    full = lambda a: pl.BlockSpec(a.shape, lambda bi: (0,) * a.ndim)
    return pl.pallas_call(
        _ret_sample_kernel,
        grid=(b,),
        in_specs=[
            pl.BlockSpec((None, nq, ATT_W), lambda bi: (bi, 0, OFF_QR // ATT_W)),
            pl.BlockSpec((None, nq, ATT_W), lambda bi: (bi, 0, OFF_QR // ATT_W + 1)),
            pl.BlockSpec((None, RET_HEADS, RET_DK, RET_DV), lambda bi: (bi, 0, 0, 0)),
            full(cos), full(sin), full(inner_p), full(qd), full(kd), full(cd),
        ],
        out_specs=[
            pl.BlockSpec((None, nq, RET_V_W), lambda bi: (bi, 0, 0)),
            pl.BlockSpec((None, RET_HEADS, RET_DK, RET_DV), lambda bi: (bi, 0, 0, 0)),
        ],
        out_shape=[jax.ShapeDtypeStruct((b, nq, RET_V_W), F32),
                   jax.ShapeDtypeStruct((b, RET_HEADS, RET_DK, RET_DV), F32)],
        compiler_params=_cparams(("parallel",)),
        name="ret_sample",
    )(proj, proj, state, cos, sin, inner_p, qd, kd, cd)


def _branch_merge_kernel(oa_ref, or_ref, ga_ref, gb_ref, wa_ref, wr_ref, o_ref):
    a = jnp.dot(oa_ref[...], wa_ref[...], preferred_element_type=F32)
    r = jnp.dot(or_ref[...], wr_ref[...], preferred_element_type=F32)
    o_ref[...] = (_sigmoid(ga_ref[...]) * a + _sigmoid(gb_ref[...]) * r).astype(o_ref.dtype)


def _branch_merge(o_att, o_ret, proj, w_att, w_ret, tm, tn):
    t = o_att.shape[0]
    n = w_att.shape[1]
    tm = min(tm, t)
    return pl.pallas_call(
        _branch_merge_kernel,
        grid=(t // tm, n // tn),
        in_specs=[
            pl.BlockSpec((tm, o_att.shape[1]), lambda i, j: (i, 0)),
            pl.BlockSpec((tm, o_ret.shape[1]), lambda i, j: (i, 0)),
            pl.BlockSpec((tm, tn), lambda i, j: (i, OFF_GA // tn + j)),
            pl.BlockSpec((tm, tn), lambda i, j: (i, (OFF_GA + n) // tn + j)),
            pl.BlockSpec((w_att.shape[0], tn), lambda i, j: (0, j)),
            pl.BlockSpec((w_ret.shape[0], tn), lambda i, j: (0, j)),
        ],
        out_specs=pl.BlockSpec((tm, tn), lambda i, j: (i, j)),
        out_shape=jax.ShapeDtypeStruct((t, n), BF16),
        compiler_params=_cparams(("parallel", "arbitrary")),
        name="branch_merge",
    )(o_att, o_ret, proj, proj, w_att, w_ret)


def _residual_matmul_kernel(x_ref, m_ref, w_ref, o_ref):
    o_ref[...] = x_ref[...] + jnp.dot(m_ref[...], w_ref[...], preferred_element_type=F32)


def _residual_matmul(x, mixed, w, tm, tn):
    t, d = x.shape
    tm = min(tm, t)
    return pl.pallas_call(
        _residual_matmul_kernel,
        grid=(t // tm, d // tn),
        in_specs=[
            pl.BlockSpec((tm, tn), lambda i, j: (i, j)),
            pl.BlockSpec((tm, mixed.shape[1]), lambda i, j: (i, 0)),
            pl.BlockSpec((w.shape[0], tn), lambda i, j: (0, j)),
        ],
        out_specs=pl.BlockSpec((tm, tn), lambda i, j: (i, j)),
        out_shape=jax.ShapeDtypeStruct((t, d), F32),
        compiler_params=_cparams(("parallel", "arbitrary")),
        name="mix_out_residual",
    )(x, mixed, w)


def _extract_top(x, n):
    tops = []
    for _ in range(n):
        m = jnp.max(x, axis=0, keepdims=True)
        tops.append(m)
        x = jnp.where(x == m, F32_MIN, x)
    return tops


def _peer_route_kernel(h_ref, nw_ref, wq_ref, sk_ref, hn_ref, s1_ref, s2_ref, e1_ref, e2_ref, tau_ref):
    x = h_ref[...]
    ms = jnp.mean(x * x, axis=-1, keepdims=True)
    hn = (x * lax.rsqrt(ms + EPS) * nw_ref[...]).astype(BF16)
    hn_ref[...] = hn
    q = jnp.dot(hn, wq_ref[...], preferred_element_type=F32).astype(BF16)
    half = PEER_NKEYS
    for h in range(PEER_HEADS):
        tops = []
        for p in range(2):
            hp = 2 * h + p
            s_t = lax.dot_general(sk_ref[hp], q[:, hp * half:(hp + 1) * half], (((1,), (1,)), ((), ())),
                                  preferred_element_type=F32)
            top = _extract_top(s_t, PEER_TOPK + 1)
            tops.append(top)
            e = jnp.exp(s_t - top[0])
            if p == 0:
                s1_ref[h] = s_t
                e1_ref[h] = e
            else:
                s2_ref[h] = s_t
                e2_raw = e
        t2 = jnp.concatenate(tops[1][:PEER_TOPK], axis=0)
        t2_head = jnp.concatenate(tops[1][:8], axis=0)
        cand = jnp.concatenate([tops[0][0] + t2] + [tops[0][a] + t2_head for a in range(1, PEER_TOPK)],
                               axis=0)
        best = _extract_top(cand, PEER_TOPK + 1)
        z = jnp.zeros_like(best[0])
        for c in best[:PEER_TOPK]:
            z = z + jnp.exp(c - best[0])
        e2_ref[h] = e2_raw / z
        runner_up = jnp.maximum(best[PEER_TOPK], jnp.maximum(tops[0][PEER_TOPK] + tops[1][0],
                                                             tops[0][0] + tops[1][PEER_TOPK]))
        tau_ref[pl.ds(h, 1), :] = 0.5 * (best[PEER_TOPK - 1] + runner_up)


def _peer_route(h, norm_w, wq, sub_keys, tm):
    t, d = h.shape
    tm = min(tm, t)
    stat = jax.ShapeDtypeStruct((PEER_HEADS, PEER_NKEYS, t), F32)
    stat_spec = pl.BlockSpec((PEER_HEADS, PEER_NKEYS, tm), lambda i: (0, 0, i))
    return pl.pallas_call(
        _peer_route_kernel,
        grid=(t // tm,),
        in_specs=[
            pl.BlockSpec((tm, d), lambda i: (i, 0)),
            pl.BlockSpec((1, d), lambda i: (0, 0)),
            pl.BlockSpec(wq.shape, lambda i: (0, 0)),
            pl.BlockSpec(sub_keys.shape, lambda i: (0, 0, 0)),
        ],
        out_specs=[pl.BlockSpec((tm, d), lambda i: (i, 0))] + [stat_spec] * 4
        + [pl.BlockSpec((PEER_HEADS, tm), lambda i: (0, i))],
        out_shape=[jax.ShapeDtypeStruct((t, d), BF16)] + [stat] * 4
        + [jax.ShapeDtypeStruct((PEER_HEADS, t), F32)],
        compiler_params=_cparams(("parallel",)),
        name="peer_route",
    )(h, norm_w.reshape(1, d), wq, sub_keys)


def _gelu_tanh(x):
    c = math.sqrt(2.0 / math.pi)
    inner = x * (c + (c * 0.044715) * (x * x))
    half = 0.5 * x
    return half + half * jnp.tanh(inner)


def _peer_expert_kernel(hn_ref, u_ref, v_ref, s1_ref, e1_ref, s2_ref, e2_ref, tau_ref, h_ref, y_ref,
                        act_s, g_s):
    e_tile = u_ref.shape[0]
    tm = hn_ref.shape[0]
    nk = PEER_NKEYS
    lc = min(LANE_CHUNK, tm)

    @pl.when(pl.program_id(1) == 0)
    def _():
        y_ref[...] = h_ref[...]

    act_s[...] = lax.dot_general(u_ref[...], hn_ref[...], (((1,), (1,)), ((), ())),
                                 preferred_element_type=F32)

    def body(a, carry):
        rows = pl.ds(pl.multiple_of(a * nk, nk), nk)
        thr = [tau_ref[h:h + 1, :] - s1_ref[h, pl.ds(a, 1), :] for h in range(PEER_HEADS)]
        e1rows = [e1_ref[h, pl.ds(a, 1), :] for h in range(PEER_HEADS)]
        for tcol in range(tm // lc):
            lanes = slice(tcol * lc, (tcol + 1) * lc)
            w = jnp.zeros((nk, lc), F32)
            for h in range(PEER_HEADS):
                gate = e2_ref[h, :, lanes] * e1rows[h][:, lanes]
                w = w + jnp.where(s2_ref[h, :, lanes] >= thr[h][:, lanes], gate, 0.0)
            g_s[rows, lanes] = (w * _gelu_tanh(act_s[rows, lanes])).astype(BF16)
        return carry

    lax.fori_loop(0, e_tile // nk, body, 0)
    y_ref[...] += lax.dot_general(g_s[...], v_ref[...], (((0,), (0,)), ((), ())),
                                  preferred_element_type=F32)


def _peer_expert(hn, u, v, s1, s2, e1, e2, tau, h, tm, e_tile):
    t, d = h.shape
    tm = min(tm, t)
    n_exp = u.shape[0]
    a_tile = e_tile // PEER_NKEYS
    row_spec = pl.BlockSpec((tm, d), lambda i, e: (i, 0))
    a_spec = pl.BlockSpec((PEER_HEADS, a_tile, tm), lambda i, e: (0, e, i))
    b_spec = pl.BlockSpec((PEER_HEADS, PEER_NKEYS, tm), lambda i, e: (0, 0, i))
    return pl.pallas_call(
        _peer_expert_kernel,
        grid=(t // tm, n_exp // e_tile),
        in_specs=[
            row_spec,
            pl.BlockSpec((e_tile, d), lambda i, e: (e, 0)),
            pl.BlockSpec((e_tile, d), lambda i, e: (e, 0)),
            a_spec, a_spec, b_spec, b_spec,
            pl.BlockSpec((PEER_HEADS, tm), lambda i, e: (0, i)),
            row_spec,
        ],
        out_specs=row_spec,
        out_shape=jax.ShapeDtypeStruct((t, d), F32),
        scratch_shapes=[pltpu.VMEM((e_tile, tm), F32), pltpu.VMEM((e_tile, tm), BF16)],
        compiler_params=_cparams(("parallel", "arbitrary")),
        name="peer_expert",
    )(hn, u, v, s1, e1, s2, e2, tau, h)


def _token_tail(x, proj, o_att, o_ret, w, tm):
    mixed = _branch_merge(o_att, o_ret, proj, w["att_br"], w["ret_br"], tm, 512)
    h = _residual_matmul(x, mixed, w["mix_out"], tm, 512)
    hn, s1, s2, e1, e2, tau = _peer_route(h, w["norm2"], w["peer_q"], w["sub_keys"], 256)
    return _peer_expert(hn, w["peer_u"], w["peer_v"], s1, s2, e1, e2, tau, h, 512, 1024)


def kernel(x_prompt, x_sample, cache_att_w128, cache_att_w512, cache_att_w2048, state_ret, norm1_w, w_in,
           att_q_norm_w, att_k_norm_w, w_att_br, w_ret_br, w_mix_out, norm2_w, peer_w_q, peer_sub_keys,
           peer_u, peer_v):
    depth = w_in.shape[0]
    assert depth == 1, "single-layer trunk"
    b, l, d = x_prompt.shape
    sb, sl, _ = x_sample.shape
    past = cache_att_w2048.shape[2]
    w = {
        "att_br": w_att_br[0].astype(BF16),
        "ret_br": w_ret_br[0].astype(BF16),
        "mix_out": w_mix_out[0].astype(BF16),
        "norm2": norm2_w[0],
        "peer_q": peer_w_q[0].astype(BF16),
        "sub_keys": peer_sub_keys[0].reshape(PEER_HEADS * 2, PEER_NKEYS, -1).astype(BF16),
        "peer_u": peer_u[0].astype(BF16),
        "peer_v": peer_v[0].astype(BF16),
    }
    w_in_b = w_in[0].astype(BF16)
    qn = att_q_norm_w[0].reshape(1, HD)
    kn = att_k_norm_w[0].reshape(1, HD)
    xp = x_prompt.reshape(b * l, d)
    xs = x_sample.reshape(sb * sl, d)

    proj_p = _norm_matmul(xp, norm1_w[0], w_in_b, 1024, 1024)
    proj_s = _norm_matmul(xs, norm1_w[0], w_in_b, 1024, 1024)
    in_w = proj_p.shape[1]

    cos_p, sin_p = _rope_tables(jnp.arange(l, dtype=jnp.int32), HD)
    kv_p, o_att_p = _attn_prompt(proj_p.reshape(b, l, in_w), cos_p, sin_p, qn, kn)
    o_ret_p, st_p = _ret_prompt(proj_p.reshape(b, l, in_w), cos_p, sin_p)
    y_p = _token_tail(xp, proj_p, o_att_p.reshape(b * l, ATT_OUT), o_ret_p.reshape(b * l, RET_V_W), w, 1024)

    cos_s, sin_s = _rope_tables(past + jnp.arange(sl, dtype=jnp.int32), HD)
    kvw = 2 * ATT_OUT
    c2 = cache_att_w2048.reshape(1, sb, past // 16, 16, 2, ATT_HEADS, HD)
    kv_s, o_att_s = _attn_sample(proj_s.reshape(sb, sl, in_w), cache_att_w128, cache_att_w512, c2,
                                 cos_s, sin_s, qn, kn)
    o_ret_s, st_s = _ret_sample(proj_s.reshape(sb, sl, in_w), state_ret[0], cos_s, sin_s)
    y_s = _token_tail(xs, proj_s, o_att_s.reshape(sb * sl, ATT_OUT).astype(BF16),
                      o_ret_s.reshape(sb * sl, RET_V_W).astype(BF16), w, 1024)

    new_p = []
    for g, (win, _) in enumerate(ATT_GROUPS):
        rows = min(win, l)
        kk = kv_p[2 * g][:, l - rows:].reshape(b, rows, 1, ATT_HEADS, HD)
        vv = kv_p[2 * g + 1][:, l - rows:].reshape(b, rows, 1, ATT_HEADS, HD)
        new_p.append(jnp.concatenate([kk, vv], axis=2)[None])
    new_s = [kv.reshape(1, sb, sl, 2, ATT_HEADS, HD) for kv in kv_s]
    return (y_p.reshape(b, l, d), y_s.reshape(sb, sl, d),
            new_p[0], new_p[1], new_p[2], st_p[None],
            new_s[0], new_s[1], new_s[2], st_s[None])
```

```python
import functools
import math

import jax
import jax.numpy as jnp
from jax import lax
from jax.experimental import pallas as pl
from jax.experimental.pallas import tpu as pltpu

F32 = jnp.float32
BF16 = jnp.bfloat16

ATT_GROUPS = ((128, 1), (512, 4), (2048, 16))
N_GROUPS = 3
ATT_HEADS = 8
HD = 128
ATT_W = N_GROUPS * ATT_HEADS * HD
ATT_OUT = ATT_HEADS * HD
N_KEYS = 129
ROPE_THETA = 10000.0
RET_HEADS = 8
RET_DK = 128
RET_DV = 256
RET_QK_W = RET_HEADS * RET_DK
RET_V_W = RET_HEADS * RET_DV
RET_CHUNK = 128
PEER_HEADS = 8
PEER_NKEYS = 128
PEER_TOPK = 16
EPS = 1e-6
NEG = -1e30
F32_MIN = float(jnp.finfo(jnp.float32).min)

OFF_QA = 0
OFF_KA = ATT_W
OFF_VA = 2 * ATT_W
OFF_QR = 3 * ATT_W
OFF_KR = OFF_QR + RET_QK_W
OFF_VR = OFF_KR + RET_QK_W
OFF_GR = OFF_VR + RET_V_W
OFF_GA = OFF_GR + RET_V_W

RET_HEADS_PER_STEP = 2
BLOCK_UNROLL = 8
SLOT_BLOCK = 4
LANE_CHUNK = 256
VMEM_LIMIT = 56 * 1024 * 1024


def _cparams(sem):
    return pltpu.CompilerParams(dimension_semantics=sem, vmem_limit_bytes=VMEM_LIMIT)


def _sigmoid(x):
    return 1.0 / (1.0 + jnp.exp(-x))


def _rope_tables(pos, width):
    half = width // 2
    inv_freq = jnp.power(jnp.float32(ROPE_THETA), -jnp.arange(half, dtype=F32) / half)
    ang = pos.astype(F32)[:, None] * inv_freq[None, :]
    cos, sin = jnp.cos(ang), jnp.sin(ang)
    return jnp.concatenate([cos, cos], axis=-1), jnp.concatenate([-sin, sin], axis=-1)


def _rope(x, cos, sin):
    return x * cos + pltpu.roll(x, HD // 2, 1) * sin


def _head_rmsnorm(x, w):
    ms = jnp.mean(x * x, axis=-1, keepdims=True)
    return x * lax.rsqrt(ms + EPS) * w


def _norm_matmul_kernel(x_ref, nw_ref, w_ref, o_ref, xn_ref):
    @pl.when(pl.program_id(1) == 0)
    def _():
        x = x_ref[...]
        ms = jnp.mean(x * x, axis=-1, keepdims=True)
        xn_ref[...] = (x * lax.rsqrt(ms + EPS) * nw_ref[...]).astype(BF16)

    o_ref[...] = jnp.dot(xn_ref[...], w_ref[...], preferred_element_type=F32)


def _norm_matmul(x, norm_w, w_bf16, tm, tn):
    t, d = x.shape
    n = w_bf16.shape[1]
    tm = min(tm, t)
    return pl.pallas_call(
        _norm_matmul_kernel,
        grid=(t // tm, n // tn),
        in_specs=[
            pl.BlockSpec((tm, d), lambda i, j: (i, 0)),
            pl.BlockSpec((1, d), lambda i, j: (0, 0)),
            pl.BlockSpec((d, tn), lambda i, j: (0, j)),
        ],
        out_specs=pl.BlockSpec((tm, tn), lambda i, j: (i, j)),
        out_shape=jax.ShapeDtypeStruct((t, n), F32),
        scratch_shapes=[pltpu.VMEM((tm, d), BF16)],
        compiler_params=_cparams(("parallel", "arbitrary")),
        name="norm_inproj",
    )(x, norm_w.reshape(1, d), w_bf16)


def _attn_prompt_kernel(q0, k0, v0, q1, k1, v1, q2, k2, v2, cos_ref, sin_ref, qn_ref, kn_ref,
                        ko0, vo0, ko1, vo1, ko2, vo2, o_ref,
                        q_s, k_s, v_s, og_s, lse_s):
    L = q0.shape[0]
    blk = 128
    scale = HD ** -0.5
    cos = cos_ref[...]
    sin = sin_ref[...]
    k_s[pl.ds(0, L), :] = jnp.zeros((L, HD), F32)
    v_s[pl.ds(0, L), :] = jnp.zeros((L, HD), F32)
    groups = ((q0, k0, v0, ko0, vo0), (q1, k1, v1, ko1, vo1), (q2, k2, v2, ko2, vo2))
    for g, (qr, kr, vr, ko, vo) in enumerate(groups):
        win, dil = ATT_GROUPS[g]
        q = _rope(_head_rmsnorm(qr[...], qn_ref[...]), cos, sin) * scale
        k = _rope(_head_rmsnorm(kr[...], kn_ref[...]), cos, sin)
        v = vr[...]
        ko[...] = k
        vo[...] = v
        q_s[...] = q
        k_s[pl.ds(L, L), :] = k
        v_s[pl.ds(L, L), :] = v
        log_d = int(math.log2(dil))

        def body(t, carry, dil=dil, log_d=log_d, g=g):
            r = jnp.bitwise_and(t, dil - 1)
            i = lax.shift_right_logical(t, log_d)
            qstart = r + dil * blk * i
            kstart = L + qstart - dil * blk
            if dil == 1:
                qsl = pl.ds(qstart, blk)
                ksl = pl.ds(kstart, 2 * blk)
            else:
                qsl = pl.ds(qstart, blk, stride=dil)
                ksl = pl.ds(kstart, 2 * blk, stride=dil)
            qb = q_s[qsl, :].astype(BF16)
            kb = k_s[ksl, :].astype(BF16)
            vb = v_s[ksl, :].astype(BF16)
            s = lax.dot_general(qb, kb, (((1,), (1,)), ((), ())), preferred_element_type=F32)
            qi = lax.broadcasted_iota(jnp.int32, (blk, 2 * blk), 0)
            kj = lax.broadcasted_iota(jnp.int32, (blk, 2 * blk), 1)
            lo = jnp.maximum(qi, jnp.where(i == 0, blk, 0))
            valid = jnp.logical_and(kj >= lo, kj <= qi + blk)
            s = jnp.where(valid, s, NEG)
            m = jnp.max(s, axis=-1, keepdims=True)
            p = jnp.exp(s - m)
            den = jnp.sum(p, axis=-1, keepdims=True)
            o = jnp.dot((p / den).astype(BF16), vb, preferred_element_type=F32)
            lse = m + jnp.log(den)
            osl = pl.ds(g * L + qstart, blk) if dil == 1 else pl.ds(g * L + qstart, blk, stride=dil)
            og_s[osl, :] = o
            lse_s[osl, :] = jnp.broadcast_to(lse, (blk, HD))
            return carry

        lax.fori_loop(0, L // blk, body, 0, unroll=True)

    l0 = lse_s[pl.ds(0, L), :]
    l1 = lse_s[pl.ds(L, L), :]
    l2 = lse_s[pl.ds(2 * L, L), :]
    mx = jnp.maximum(jnp.maximum(l0, l1), l2)
    e0 = jnp.exp(l0 - mx)
    e1 = jnp.exp(l1 - mx)
    e2 = jnp.exp(l2 - mx)
    tot = e0 + e1 + e2
    o = ((e0 / tot) * og_s[pl.ds(0, L), :] + (e1 / tot) * og_s[pl.ds(L, L), :]
         + (e2 / tot) * og_s[pl.ds(2 * L, L), :])
    o_ref[...] = o.astype(o_ref.dtype)


def _attn_prompt(proj, cos, sin, qn, kn):
    b, l, _ = proj.shape
    in_specs = []
    for g in range(N_GROUPS):
        for off in (OFF_QA, OFF_KA, OFF_VA):
            cb = (off + g * ATT_OUT) // HD
            in_specs.append(pl.BlockSpec((None, l, HD), lambda bi, hi, cb=cb: (bi, 0, cb + hi)))
    in_specs += [
        pl.BlockSpec((l, HD), lambda bi, hi: (0, 0)),
        pl.BlockSpec((l, HD), lambda bi, hi: (0, 0)),
        pl.BlockSpec((1, HD), lambda bi, hi: (0, 0)),
        pl.BlockSpec((1, HD), lambda bi, hi: (0, 0)),
    ]
    head_spec = pl.BlockSpec((None, l, HD), lambda bi, hi: (bi, 0, hi))
    kv_shape = jax.ShapeDtypeStruct((b, l, ATT_OUT), F32)
    outs = pl.pallas_call(
        _attn_prompt_kernel,
        grid=(b, ATT_HEADS),
        in_specs=in_specs,
        out_specs=[head_spec] * 7,
        out_shape=[kv_shape] * 6 + [jax.ShapeDtypeStruct((b, l, ATT_OUT), BF16)],
        scratch_shapes=[
            pltpu.VMEM((l, HD), F32),
            pltpu.VMEM((2 * l, HD), F32),
            pltpu.VMEM((2 * l, HD), F32),
            pltpu.VMEM((N_GROUPS * l, HD), F32),
            pltpu.VMEM((N_GROUPS * l, HD), F32),
        ],
        compiler_params=_cparams(("parallel", "arbitrary")),
        name="attn_prompt",
    )(*([proj] * 9), cos, sin, qn, kn)
    return outs[:6], outs[6]


def _lane_sum(x):
    return jnp.broadcast_to(jnp.sum(x, axis=-1, keepdims=True), x.shape)


def _attn_sample_kernel(qa_ref, ka_ref, va_ref, c0_ref, c1_ref, c2_ref, cos_ref, sin_ref,
                        qn_ref, kn_ref, kv0_ref, kv1_ref, kv2_ref, o_ref, tile_q, tile_k, tile_v):
    nq = qa_ref.shape[0]
    scale = HD ** -0.5 * math.log2(math.e)
    cos = cos_ref[...]
    sin = sin_ref[...]
    qn = qn_ref[...]
    kn = kn_ref[...]
    caches = (c0_ref, c1_ref, c2_ref)
    kv_refs = (kv0_ref, kv1_ref, kv2_ref)

    def head_tiles(x, tile_s):
        for h in range(ATT_HEADS):
            tile_s[pl.ds(h, nq, stride=ATT_HEADS), :] = x[:, h * HD:(h + 1) * HD]
        return [tile_s[i * ATT_HEADS:(i + 1) * ATT_HEADS, :] for i in range(nq)]

    o_groups = []
    lse_groups = []
    for g in range(N_GROUPS):
        win, dil = ATT_GROUPS[g]
        c_ref = caches[g]
        qs, ks = [], []
        for h in range(ATT_HEADS):
            col = g * ATT_OUT + h * HD
            qs.append(_rope(_head_rmsnorm(qa_ref[:, col:col + HD], qn), cos, sin) * scale)
            ks.append(_rope(_head_rmsnorm(ka_ref[:, col:col + HD], kn), cos, sin))
        q_g = jnp.concatenate(qs, axis=1)
        k_g = jnp.concatenate(ks, axis=1)
        v_g = va_ref[:, g * ATT_OUT:(g + 1) * ATT_OUT]
        kv_refs[g][:, 0:ATT_OUT] = k_g
        kv_refs[g][:, ATT_OUT:2 * ATT_OUT] = v_g
        qt = head_tiles(q_g, tile_q)
        kt = head_tiles(k_g, tile_k)
        vt = head_tiles(v_g, tile_v)

        if g == 2:
            n_slots = c_ref.shape[0]
            load = lambda jj, kv, n: c_ref[jj, :, kv]
            n_valid = lambda jj: nq
        else:
            n_slots = win // dil
            load = lambda jj, kv, n, dil=dil: c_ref[pl.ds(dil * jj, n), kv]
            n_valid = lambda jj, dil=dil, win=win: min(nq, win - dil * jj)
        n_main = max(jj + 1 for jj in range(n_slots) if n_valid(jj) == nq)
        new_keys = [list(range(i, -1, -dil)) for i in range(nq)]

        def update(state, scores, values):
            m, den, acc = state
            m_new = jnp.maximum(m, functools.reduce(jnp.maximum, scores))
            a = jnp.exp2(m - m_new)
            ps = [jnp.exp2(s - m_new) for s in scores]
            den = den * a + functools.reduce(lambda x, y: x + y, ps)
            acc = acc * a + functools.reduce(lambda x, y: x + y, [p * v for p, v in zip(ps, values)])
            return m_new, den, acc

        states = []
        for i in range(nq):
            t0 = new_keys[i][0]
            st = (_lane_sum(qt[i] * kt[t0]), jnp.ones((ATT_HEADS, HD), F32), vt[t0])
            if len(new_keys[i]) > 1:
                ts = new_keys[i][1:]
                st = update(st, [_lane_sum(qt[i] * kt[t]) for t in ts], [vt[t] for t in ts])
            states.append(st)

        def block(slots, states):
            scores = [[] for _ in range(nq)]
            values = [[] for _ in range(nq)]
            for jj, valid in slots:
                kt8 = load(jj, 0, valid)
                vt8 = load(jj, 1, valid)
                for i in range(valid):
                    scores[i].append(_lane_sum(qt[i] * kt8[i]))
                    values[i].append(vt8[i])
            return tuple(update(states[i], scores[i], values[i]) if scores[i] else states[i]
                         for i in range(nq))

        n_blocks = n_main // SLOT_BLOCK
        states = lax.fori_loop(
            0, n_blocks,
            lambda bi, st: block([(bi * SLOT_BLOCK + j, nq) for j in range(SLOT_BLOCK)], st),
            tuple(states), unroll=BLOCK_UNROLL)
        rest = [(jj, n_valid(jj)) for jj in range(n_blocks * SLOT_BLOCK, n_slots)]
        for k in range(0, len(rest), SLOT_BLOCK):
            states = block(rest[k:k + SLOT_BLOCK], states)
        o_groups.append([acc / den for (_, den, acc) in states])
        lse_groups.append([m * math.log(2.0) + jnp.log(den) for (m, den, _) in states])

    for i in range(nq):
        l0, l1, l2 = lse_groups[0][i], lse_groups[1][i], lse_groups[2][i]
        mx = jnp.maximum(jnp.maximum(l0, l1), l2)
        e0, e1, e2 = jnp.exp(l0 - mx), jnp.exp(l1 - mx), jnp.exp(l2 - mx)
        tot = e0 + e1 + e2
        o_ref[i] = ((e0 / tot) * o_groups[0][i] + (e1 / tot) * o_groups[1][i]
                    + (e2 / tot) * o_groups[2][i])


def _attn_sample(proj, c0, c1, c2, cos, sin, qn, kn):
    b, nq, _ = proj.shape
    kvw = 2 * ATT_OUT
    small = lambda shape: pl.BlockSpec(shape, lambda bi: (0, 0))
    tail = (2, ATT_HEADS, HD)
    outs = pl.pallas_call(
        _attn_sample_kernel,
        grid=(b,),
        in_specs=[
            pl.BlockSpec((None, nq, ATT_W), lambda bi: (bi, 0, OFF_QA // ATT_W)),
            pl.BlockSpec((None, nq, ATT_W), lambda bi: (bi, 0, OFF_KA // ATT_W)),
            pl.BlockSpec((None, nq, ATT_W), lambda bi: (bi, 0, OFF_VA // ATT_W)),
            pl.BlockSpec((None, None, c0.shape[2]) + tail, lambda bi: (0, bi, 0, 0, 0, 0)),
            pl.BlockSpec((None, None, c1.shape[2]) + tail, lambda bi: (0, bi, 0, 0, 0, 0)),
            pl.BlockSpec((None, None, c2.shape[2], nq) + tail, lambda bi: (0, bi, 0, 0, 0, 0, 0)),
            small((nq, HD)), small((nq, HD)), small((1, HD)), small((1, HD)),
        ],
        out_specs=[pl.BlockSpec((None, nq, kvw), lambda bi: (bi, 0, 0))] * 3
        + [pl.BlockSpec((None, nq, ATT_HEADS, HD), lambda bi: (bi, 0, 0, 0))],
        out_shape=[jax.ShapeDtypeStruct((b, nq, kvw), F32)] * 3
        + [jax.ShapeDtypeStruct((b, nq, ATT_HEADS, HD), F32)],
        scratch_shapes=[pltpu.VMEM((nq * ATT_HEADS, HD), F32)] * 3,
        compiler_params=_cparams(("parallel",)),
        name="attn_sample",
    )(proj, proj, proj, c0, c1, c2, cos, sin, qn, kn)
    return outs[:3], outs[3]


def _ret_tables(c):
    log_g = jnp.log1p(-jnp.exp2(-5.0 - jnp.arange(RET_HEADS, dtype=F32)))
    ii = jnp.arange(c, dtype=F32)
    diff = ii[:, None] - ii[None, :]
    inner = jnp.where(diff[None] >= 0, jnp.exp(jnp.maximum(diff, 0.0)[None] * log_g[:, None, None]), 0.0)
    q_dec = jnp.exp((ii + 1.0)[None, :] * log_g[:, None])
    k_dec = jnp.exp((c - 1.0 - ii)[None, :] * log_g[:, None])
    chunk_dec = jnp.exp(c * log_g)
    return inner, q_dec, k_dec, chunk_dec


def _ret_prompt_kernel(q_ref, k_ref, v_ref, g_ref, cos_ref, sin_ref, inner_ref, qd_ref, kd_ref, cd_ref,
                       o_ref, st_ref, q_s, k_s, s_s):
    L = q_ref.shape[0]
    C = RET_CHUNK
    nh = RET_HEADS_PER_STEP
    cos = cos_ref[...]
    sin = sin_ref[...]
    for hh in range(nh):
        qk = slice(hh * RET_DK, (hh + 1) * RET_DK)
        q_s[:, qk] = _rope(q_ref[:, qk], cos, sin)
        k_s[:, qk] = _rope(k_ref[:, qk], cos, sin) * (RET_DK ** -0.5)
    s_s[...] = jnp.zeros(s_s.shape, F32)

    def body(ci, carry):
        sl = pl.ds(pl.multiple_of(ci * C, C), C)
        for hh in range(nh):
            qk = slice(hh * RET_DK, (hh + 1) * RET_DK)
            vv = slice(hh * RET_DV, (hh + 1) * RET_DV)
            qc = q_s[sl, qk]
            kc = k_s[sl, qk]
            vc = v_ref[sl, vv].astype(BF16)
            state = s_s[hh]
            sc = lax.dot_general(qc.astype(BF16), kc.astype(BF16), (((1,), (1,)), ((), ())),
                                 preferred_element_type=F32) * inner_ref[hh]
            o = (jnp.dot(sc.astype(BF16), vc, preferred_element_type=F32)
                 + jnp.dot((qc * qd_ref[hh]).astype(BF16), state.astype(BF16), preferred_element_type=F32))
            kdt = jnp.transpose(kc * kd_ref[hh]).astype(BF16)
            s_s[hh] = state * cd_ref[hh] + jnp.dot(kdt, vc, preferred_element_type=F32)
            ms = jnp.mean(o * o, axis=-1, keepdims=True)
            gate = g_ref[sl, vv]
            o_ref[sl, vv] = (o * lax.rsqrt(ms + EPS) * (gate * _sigmoid(gate))).astype(o_ref.dtype)
        return carry

    lax.fori_loop(0, L // C, body, 0)
    st_ref[...] = s_s[...]


def _ret_prompt(proj, cos, sin):
    b, l, _ = proj.shape
    nh = RET_HEADS_PER_STEP
    inner, q_dec, k_dec, chunk_dec = _ret_tables(RET_CHUNK)
    qd = jnp.broadcast_to(q_dec[:, :, None], (RET_HEADS, RET_CHUNK, RET_DK))
    kd = jnp.broadcast_to(k_dec[:, :, None], (RET_HEADS, RET_CHUNK, RET_DK))
    cd = jnp.broadcast_to(chunk_dec[:, None, None], (RET_HEADS, RET_DK, RET_DV))
    per_head = lambda r, c: pl.BlockSpec((nh, r, c), lambda bi, hi: (hi, 0, 0))
    qk_w, v_w = nh * RET_DK, nh * RET_DV
    return pl.pallas_call(
        _ret_prompt_kernel,
        grid=(b, RET_HEADS // nh),
        in_specs=[
            pl.BlockSpec((None, l, qk_w), lambda bi, hi: (bi, 0, OFF_QR // qk_w + hi)),
            pl.BlockSpec((None, l, qk_w), lambda bi, hi: (bi, 0, OFF_KR // qk_w + hi)),
            pl.BlockSpec((None, l, v_w), lambda bi, hi: (bi, 0, OFF_VR // v_w + hi)),
            pl.BlockSpec((None, l, v_w), lambda bi, hi: (bi, 0, OFF_GR // v_w + hi)),
            pl.BlockSpec((l, RET_DK), lambda bi, hi: (0, 0)),
            pl.BlockSpec((l, RET_DK), lambda bi, hi: (0, 0)),
            per_head(RET_CHUNK, RET_CHUNK), per_head(RET_CHUNK, RET_DK), per_head(RET_CHUNK, RET_DK),
            per_head(RET_DK, RET_DV),
        ],
        out_specs=[
            pl.BlockSpec((None, l, v_w), lambda bi, hi: (bi, 0, hi)),
            pl.BlockSpec((None, nh, RET_DK, RET_DV), lambda bi, hi: (bi, hi, 0, 0)),
        ],
        out_shape=[jax.ShapeDtypeStruct((b, l, RET_V_W), BF16),
                   jax.ShapeDtypeStruct((b, RET_HEADS, RET_DK, RET_DV), F32)],
        scratch_shapes=[pltpu.VMEM((l, qk_w), F32), pltpu.VMEM((l, qk_w), F32),
                        pltpu.VMEM((nh, RET_DK, RET_DV), F32)],
        compiler_params=_cparams(("parallel", "arbitrary")),
        name="ret_prompt",
    )(proj, proj, proj, proj, cos, sin, inner, qd, kd, cd)


def _ret_sample_kernel(a_ref, b_ref, s0_ref, cos_ref, sin_ref, inner_ref, qd_ref, kd_ref, cd_ref,
                       o_ref, st_ref):
    nq = a_ref.shape[0]
    cos = cos_ref[...]
    sin = sin_ref[...]
    pad = 128 - nq
    for h in range(RET_HEADS):
        q = _rope(a_ref[:, h * RET_DK:(h + 1) * RET_DK], cos, sin)
        k = _rope(a_ref[:, RET_QK_W + h * RET_DK:RET_QK_W + (h + 1) * RET_DK], cos, sin) * (RET_DK ** -0.5)
        vc0 = h * RET_DV
        if vc0 < ATT_OUT:
            v = a_ref[:, 2 * RET_QK_W + vc0:2 * RET_QK_W + vc0 + RET_DV]
        else:
            v = b_ref[:, vc0 - ATT_OUT:vc0 - ATT_OUT + RET_DV]
        gate = b_ref[:, ATT_OUT + h * RET_DV:ATT_OUT + (h + 1) * RET_DV]
        state = s0_ref[h]
        kp = jnp.concatenate([k, jnp.zeros((pad, RET_DK), F32)], axis=0)
        vp = jnp.concatenate([v, jnp.zeros((pad, RET_DV), F32)], axis=0).astype(BF16)
        sc = lax.dot_general(q.astype(BF16), kp.astype(BF16), (((1,), (1,)), ((), ())),
                             preferred_element_type=F32) * inner_ref[h]
        o = (jnp.dot(sc.astype(BF16), vp, preferred_element_type=F32)
             + jnp.dot((q * qd_ref[h]).astype(BF16), state.astype(BF16), preferred_element_type=F32))
        kdt = jnp.transpose(kp * kd_ref[h]).astype(BF16)
        st_ref[h] = state * cd_ref[h] + jnp.dot(kdt, vp, preferred_element_type=F32)
        ms = jnp.mean(o * o, axis=-1, keepdims=True)
        o_ref[:, h * RET_DV:(h + 1) * RET_DV] = o * lax.rsqrt(ms + EPS) * (gate * _sigmoid(gate))


def _ret_sample(proj, state, cos, sin):
    b, nq, _ = proj.shape
    inner, q_dec, k_dec, chunk_dec = _ret_tables(math.gcd(nq, RET_CHUNK))
    pad = 128 - nq
    inner_p = jnp.pad(inner, ((0, 0), (0, 0), (0, pad)))
    qd = jnp.broadcast_to(q_dec[:, :, None], (RET_HEADS, nq, RET_DK))
    kd = jnp.broadcast_to(jnp.pad(k_dec, ((0, 0), (0, pad)))[:, :, None], (RET_HEADS, 128, RET_DK))
    cd = jnp.broadcast_to(chunk_dec[:, None, None], (RET_HEADS, RET_DK, RET_DV))
    full = lambda a: pl.BlockSpec(a.shape, lambda bi: (0,) * a.ndim)
    return pl.pallas_call(
        _ret_sample_kernel,
        grid=(b,),
        in_specs=[
            pl.BlockSpec((None, nq, ATT_W), lambda bi: (bi, 0, OFF_QR // ATT_W)),
            pl.BlockSpec((None, nq, ATT_W), lambda bi: (bi, 0, OFF_QR // ATT_W + 1)),
            pl.BlockSpec((None, RET_HEADS, RET_DK, RET_DV), lambda bi: (bi, 0, 0, 0)),
            full(cos), full(sin), full(inner_p), full(qd), full(kd), full(cd),
        ],
        out_specs=[
            pl.BlockSpec((None, nq, RET_V_W), lambda bi: (bi, 0, 0)),
            pl.BlockSpec((None, RET_HEADS, RET_DK, RET_DV), lambda bi: (bi, 0, 0, 0)),
        ],
        out_shape=[jax.ShapeDtypeStruct((b, nq, RET_V_W), F32),
                   jax.ShapeDtypeStruct((b, RET_HEADS, RET_DK, RET_DV), F32)],
        compiler_params=_cparams(("parallel",)),
        name="ret_sample",
    )(proj, proj, state, cos, sin, inner_p, qd, kd, cd)


def _branch_merge_kernel(oa_ref, or_ref, ga_ref, gb_ref, wa_ref, wr_ref, o_ref):
    a = jnp.dot(oa_ref[...], wa_ref[...], preferred_element_type=F32)
    r = jnp.dot(or_ref[...], wr_ref[...], preferred_element_type=F32)
    o_ref[...] = (_sigmoid(ga_ref[...]) * a + _sigmoid(gb_ref[...]) * r).astype(o_ref.dtype)


def _branch_merge(o_att, o_ret, proj, w_att, w_ret, tm, tn):
    t = o_att.shape[0]
    n = w_att.shape[1]
    tm = min(tm, t)
    return pl.pallas_call(
        _branch_merge_kernel,
        grid=(t // tm, n // tn),
        in_specs=[
            pl.BlockSpec((tm, o_att.shape[1]), lambda i, j: (i, 0)),
            pl.BlockSpec((tm, o_ret.shape[1]), lambda i, j: (i, 0)),
            pl.BlockSpec((tm, tn), lambda i, j: (i, OFF_GA // tn + j)),
            pl.BlockSpec((tm, tn), lambda i, j: (i, (OFF_GA + n) // tn + j)),
            pl.BlockSpec((w_att.shape[0], tn), lambda i, j: (0, j)),
            pl.BlockSpec((w_ret.shape[0], tn), lambda i, j: (0, j)),
        ],
        out_specs=pl.BlockSpec((tm, tn), lambda i, j: (i, j)),
        out_shape=jax.ShapeDtypeStruct((t, n), BF16),
        compiler_params=_cparams(("parallel", "arbitrary")),
        name="branch_merge",
    )(o_att, o_ret, proj, proj, w_att, w_ret)


def _residual_matmul_kernel(x_ref, m_ref, w_ref, o_ref):
    o_ref[...] = x_ref[...] + jnp.dot(m_ref[...], w_ref[...], preferred_element_type=F32)


def _residual_matmul(x, mixed, w, tm, tn):
    t, d = x.shape
    tm = min(tm, t)
    return pl.pallas_call(
        _residual_matmul_kernel,
        grid=(t // tm, d // tn),
        in_specs=[
            pl.BlockSpec((tm, tn), lambda i, j: (i, j)),
            pl.BlockSpec((tm, mixed.shape[1]), lambda i, j: (i, 0)),
            pl.BlockSpec((w.shape[0], tn), lambda i, j: (0, j)),
        ],
        out_specs=pl.BlockSpec((tm, tn), lambda i, j: (i, j)),
        out_shape=jax.ShapeDtypeStruct((t, d), F32),
        compiler_params=_cparams(("parallel", "arbitrary")),
        name="mix_out_residual",
    )(x, mixed, w)


def _extract_top(x, n):
    tops = []
    for _ in range(n):
        m = jnp.max(x, axis=0, keepdims=True)
        tops.append(m)
        x = jnp.where(x == m, F32_MIN, x)
    return tops


def _peer_route_kernel(h_ref, nw_ref, wq_ref, sk_ref, hn_ref, s1_ref, s2_ref, e1_ref, e2_ref, tau_ref):
    x = h_ref[...]
    ms = jnp.mean(x * x, axis=-1, keepdims=True)
    hn = (x * lax.rsqrt(ms + EPS) * nw_ref[...]).astype(BF16)
    hn_ref[...] = hn
    q = jnp.dot(hn, wq_ref[...], preferred_element_type=F32).astype(BF16)
    half = PEER_NKEYS
    for h in range(PEER_HEADS):
        tops = []
        for p in range(2):
            hp = 2 * h + p
            s_t = lax.dot_general(sk_ref[hp], q[:, hp * half:(hp + 1) * half], (((1,), (1,)), ((), ())),
                                  preferred_element_type=F32)
            top = _extract_top(s_t, PEER_TOPK + 1)
            tops.append(top)
            e = jnp.exp(s_t - top[0])
            if p == 0:
                s1_ref[h] = s_t
                e1_ref[h] = e
            else:
                s2_ref[h] = s_t
                e2_raw = e
        t2 = jnp.concatenate(tops[1][:PEER_TOPK], axis=0)
        t2_head = jnp.concatenate(tops[1][:8], axis=0)
        cand = jnp.concatenate([tops[0][0] + t2] + [tops[0][a] + t2_head for a in range(1, PEER_TOPK)],
                               axis=0)
        best = _extract_top(cand, PEER_TOPK + 1)
        z = jnp.zeros_like(best[0])
        for c in best[:PEER_TOPK]:
            z = z + jnp.exp(c - best[0])
        e2_ref[h] = e2_raw / z
        runner_up = jnp.maximum(best[PEER_TOPK], jnp.maximum(tops[0][PEER_TOPK] + tops[1][0],
                                                             tops[0][0] + tops[1][PEER_TOPK]))
        tau_ref[pl.ds(h, 1), :] = 0.5 * (best[PEER_TOPK - 1] + runner_up)


def _peer_route(h, norm_w, wq, sub_keys, tm):
    t, d = h.shape
    tm = min(tm, t)
    stat = jax.ShapeDtypeStruct((PEER_HEADS, PEER_NKEYS, t), F32)
    stat_spec = pl.BlockSpec((PEER_HEADS, PEER_NKEYS, tm), lambda i: (0, 0, i))
    return pl.pallas_call(
        _peer_route_kernel,
        grid=(t // tm,),
        in_specs=[
            pl.BlockSpec((tm, d), lambda i: (i, 0)),
            pl.BlockSpec((1, d), lambda i: (0, 0)),
            pl.BlockSpec(wq.shape, lambda i: (0, 0)),
            pl.BlockSpec(sub_keys.shape, lambda i: (0, 0, 0)),
        ],
        out_specs=[pl.BlockSpec((tm, d), lambda i: (i, 0))] + [stat_spec] * 4
        + [pl.BlockSpec((PEER_HEADS, tm), lambda i: (0, i))],
        out_shape=[jax.ShapeDtypeStruct((t, d), BF16)] + [stat] * 4
        + [jax.ShapeDtypeStruct((PEER_HEADS, t), F32)],
        compiler_params=_cparams(("parallel",)),
        name="peer_route",
    )(h, norm_w.reshape(1, d), wq, sub_keys)


def _gelu_tanh(x):
    c = math.sqrt(2.0 / math.pi)
    inner = x * (c + (c * 0.044715) * (x * x))
    half = 0.5 * x
    return half + half * jnp.tanh(inner)


def _peer_expert_kernel(hn_ref, u_ref, v_ref, s1_ref, e1_ref, s2_ref, e2_ref, tau_ref, h_ref, y_ref,
                        act_s, g_s):
    e_tile = u_ref.shape[0]
    tm = hn_ref.shape[0]
    nk = PEER_NKEYS
    lc = min(LANE_CHUNK, tm)

    @pl.when(pl.program_id(1) == 0)
    def _():
        y_ref[...] = h_ref[...]

    act_s[...] = lax.dot_general(u_ref[...], hn_ref[...], (((1,), (1,)), ((), ())),
                                 preferred_element_type=F32)

    def body(a, carry):
        rows = pl.ds(pl.multiple_of(a * nk, nk), nk)
        thr = [tau_ref[h:h + 1, :] - s1_ref[h, pl.ds(a, 1), :] for h in range(PEER_HEADS)]
        e1rows = [e1_ref[h, pl.ds(a, 1), :] for h in range(PEER_HEADS)]
        for tcol in range(tm // lc):
            lanes = slice(tcol * lc, (tcol + 1) * lc)
            w = jnp.zeros((nk, lc), F32)
            for h in range(PEER_HEADS):
                gate = e2_ref[h, :, lanes] * e1rows[h][:, lanes]
                w = w + jnp.where(s2_ref[h, :, lanes] >= thr[h][:, lanes], gate, 0.0)
            g_s[rows, lanes] = (w * _gelu_tanh(act_s[rows, lanes])).astype(BF16)
        return carry

    lax.fori_loop(0, e_tile // nk, body, 0)
    y_ref[...] += lax.dot_general(g_s[...], v_ref[...], (((0,), (0,)), ((), ())),
                                  preferred_element_type=F32)


def _peer_expert(hn, u, v, s1, s2, e1, e2, tau, h, tm, e_tile):
    t, d = h.shape
    tm = min(tm, t)
    n_exp = u.shape[0]
    a_tile = e_tile // PEER_NKEYS
    row_spec = pl.BlockSpec((tm, d), lambda i, e: (i, 0))
    a_spec = pl.BlockSpec((PEER_HEADS, a_tile, tm), lambda i, e: (0, e, i))
    b_spec = pl.BlockSpec((PEER_HEADS, PEER_NKEYS, tm), lambda i, e: (0, 0, i))
    return pl.pallas_call(
        _peer_expert_kernel,
        grid=(t // tm, n_exp // e_tile),
        in_specs=[
            row_spec,
            pl.BlockSpec((e_tile, d), lambda i, e: (e, 0)),
            pl.BlockSpec((e_tile, d), lambda i, e: (e, 0)),
            a_spec, a_spec, b_spec, b_spec,
            pl.BlockSpec((PEER_HEADS, tm), lambda i, e: (0, i)),
            row_spec,
        ],
        out_specs=row_spec,
        out_shape=jax.ShapeDtypeStruct((t, d), F32),
        scratch_shapes=[pltpu.VMEM((e_tile, tm), F32), pltpu.VMEM((e_tile, tm), BF16)],
        compiler_params=_cparams(("parallel", "arbitrary")),
        name="peer_expert",
    )(hn, u, v, s1, e1, s2, e2, tau, h)


def _token_tail(x, proj, o_att, o_ret, w, tm):
    mixed = _branch_merge(o_att, o_ret, proj, w["att_br"], w["ret_br"], tm, 512)
    h = _residual_matmul(x, mixed, w["mix_out"], tm, 512)
    hn, s1, s2, e1, e2, tau = _peer_route(h, w["norm2"], w["peer_q"], w["sub_keys"], 256)
    return _peer_expert(hn, w["peer_u"], w["peer_v"], s1, s2, e1, e2, tau, h, 512, 1024)


def kernel(x_prompt, x_sample, cache_att_w128, cache_att_w512, cache_att_w2048, state_ret, norm1_w, w_in,
           att_q_norm_w, att_k_norm_w, w_att_br, w_ret_br, w_mix_out, norm2_w, peer_w_q, peer_sub_keys,
           peer_u, peer_v):
    depth = w_in.shape[0]
    assert depth == 1, "single-layer trunk"
    b, l, d = x_prompt.shape
    sb, sl, _ = x_sample.shape
    past = cache_att_w2048.shape[2]
    w = {
        "att_br": w_att_br[0].astype(BF16),
        "ret_br": w_ret_br[0].astype(BF16),
        "mix_out": w_mix_out[0].astype(BF16),
        "norm2": norm2_w[0],
        "peer_q": peer_w_q[0].astype(BF16),
        "sub_keys": peer_sub_keys[0].reshape(PEER_HEADS * 2, PEER_NKEYS, -1).astype(BF16),
        "peer_u": peer_u[0].astype(BF16),
        "peer_v": peer_v[0].astype(BF16),
    }
    w_in_b = w_in[0].astype(BF16)
    qn = att_q_norm_w[0].reshape(1, HD)
    kn = att_k_norm_w[0].reshape(1, HD)
    xp = x_prompt.reshape(b * l, d)
    xs = x_sample.reshape(sb * sl, d)

    proj_p = _norm_matmul(xp, norm1_w[0], w_in_b, 1024, 1024)
    proj_s = _norm_matmul(xs, norm1_w[0], w_in_b, 1024, 1024)
    in_w = proj_p.shape[1]

    cos_p, sin_p = _rope_tables(jnp.arange(l, dtype=jnp.int32), HD)
    kv_p, o_att_p = _attn_prompt(proj_p.reshape(b, l, in_w), cos_p, sin_p, qn, kn)
    o_ret_p, st_p = _ret_prompt(proj_p.reshape(b, l, in_w), cos_p, sin_p)
    y_p = _token_tail(xp, proj_p, o_att_p.reshape(b * l, ATT_OUT), o_ret_p.reshape(b * l, RET_V_W), w, 1024)

    cos_s, sin_s = _rope_tables(past + jnp.arange(sl, dtype=jnp.int32), HD)
    kvw = 2 * ATT_OUT
    c2 = cache_att_w2048.reshape(1, sb, past // 16, 16, 2, ATT_HEADS, HD)
    kv_s, o_att_s = _attn_sample(proj_s.reshape(sb, sl, in_w), cache_att_w128, cache_att_w512, c2,
                                 cos_s, sin_s, qn, kn)
    o_ret_s, st_s = _ret_sample(proj_s.reshape(sb, sl, in_w), state_ret[0], cos_s, sin_s)
    y_s = _token_tail(xs, proj_s, o_att_s.reshape(sb * sl, ATT_OUT).astype(BF16),
                      o_ret_s.reshape(sb * sl, RET_V_W).astype(BF16), w, 1024)

    new_p = []
    for g, (win, _) in enumerate(ATT_GROUPS):
        rows = min(win, l)
        kk = kv_p[2 * g][:, l - rows:].reshape(b, rows, 1, ATT_HEADS, HD)
        vv = kv_p[2 * g + 1][:, l - rows:].reshape(b, rows, 1, ATT_HEADS, HD)
        new_p.append(jnp.concatenate([kk, vv], axis=2)[None])
    new_s = [kv.reshape(1, sb, sl, 2, ATT_HEADS, HD) for kv in kv_s]
    return (y_p.reshape(b, l, d), y_s.reshape(sb, sl, d),
            new_p[0], new_p[1], new_p[2], st_p[None],
            new_s[0], new_s[1], new_s[2], st_s[None])
```
